```python
import jax, jax.numpy as jnp
from jax import lax
import numpy as np

D_MODEL = 1024
BATCH = 2
SEQ = 8192
DEPTH = 1
DEC_BATCH = 128
DEC_SEQ = 8
PAST_LEN = 8192
PAGE_SIZE = 128

MIX_WIDTH = D_MODEL
A_WIDTH = MIX_WIDTH // 2
B_WIDTH = MIX_WIDTH - A_WIDTH
CHUNK = 128
A_GROUPS = 4
A_GROUP_DIM = A_WIDTH // A_GROUPS
B_HEAD_DIM = 64
B_HEADS = B_WIDTH // B_HEAD_DIM
PLE_DIM = 256
Q_BLOCK = 128
RMS_EPS = 1e-6
LN_EPS = 1e-5
FORGET_BIAS_INIT = 8.0
IN_WIDTH = 3 * A_WIDTH + 4 * B_WIDTH + B_HEADS

kernel_name = "hymba_gmlp_fox_sandwich_ple_step"


def _rmsnorm(x, g):
    xf = x.astype(jnp.float32)
    y = xf * lax.rsqrt(jnp.mean(xf * xf, axis=-1, keepdims=True) + RMS_EPS) * g.astype(jnp.float32)
    return y.astype(x.dtype)


def _layernorm(x, g, b):
    xf = x.astype(jnp.float32)
    mu = jnp.mean(xf, axis=-1, keepdims=True)
    var = jnp.mean(jnp.square(xf - mu), axis=-1, keepdims=True)
    y = (xf - mu) * lax.rsqrt(var + LN_EPS) * g.astype(jnp.float32) + b.astype(jnp.float32)
    return y.astype(x.dtype)


def _split_projection(h, w_in, b_f):
    z = h @ w_in
    offsets = np.cumsum([A_WIDTH] * 3 + [B_WIDTH] * 4).tolist()
    ua, va, za, q, k, vb, zb, f = jnp.split(z, offsets, axis=-1)
    lead = h.shape[:-1]
    ua = jax.nn.gelu(ua).reshape(*lead, A_GROUPS, A_GROUP_DIM)
    va = jax.nn.gelu(va).reshape(*lead, A_GROUPS, A_GROUP_DIM)
    q = q.reshape(*lead, B_HEADS, B_HEAD_DIM)
    k = k.reshape(*lead, B_HEADS, B_HEAD_DIM)
    vb = vb.reshape(*lead, B_HEADS, B_HEAD_DIM)
    logf = jax.nn.log_sigmoid(f.astype(jnp.float32) + b_f.astype(jnp.float32))
    return ua, va, za, q, k, vb, zb, logf


def _chunk_mix_prompt(vn, w_s, b_s):
    bsz, seq = vn.shape[:2]
    mask = jnp.tril(jnp.ones((CHUNK, CHUNK), dtype=bool))
    wm = jnp.where(mask[None], w_s, 0).astype(vn.dtype)
    vc = vn.reshape(bsz, seq // CHUNK, CHUNK, A_GROUPS, A_GROUP_DIM)
    s = jnp.einsum('gts,bnsgc->bntgc', wm, vc) + b_s.T[:, :, None].astype(vn.dtype)
    return s.reshape(bsz, seq, A_GROUPS, A_GROUP_DIM)


def _chunk_mix_sample(vn, w_s, b_s):
    t = vn.shape[1]
    mask = jnp.tril(jnp.ones((t, t), dtype=bool))
    wm = jnp.where(mask[None], w_s[:, :t, :t], 0).astype(vn.dtype)
    return jnp.einsum('gts,bsgc->btgc', wm, vn) + b_s[:, :t].T[:, :, None].astype(vn.dtype)


def _fox_prompt(q, k, v, logf):
    bsz, seq = q.shape[:2]
    scale = B_HEAD_DIM ** -0.5
    c = jnp.cumsum(logf, axis=1)
    c_keys = c.transpose(0, 2, 1)
    nb = seq // Q_BLOCK
    q_blocks = q.reshape(bsz, nb, Q_BLOCK, B_HEADS, B_HEAD_DIM).transpose(1, 0, 2, 3, 4)
    c_blocks = c.reshape(bsz, nb, Q_BLOCK, B_HEADS).transpose(1, 0, 3, 2)
    key_pos = jnp.arange(seq, dtype=jnp.int32)
    starts = jnp.arange(nb, dtype=jnp.int32) * Q_BLOCK

    def block(args):
        qb, cb, start = args
        logits = (jnp.einsum('bqhd,bkhd->bhqk', qb, k).astype(jnp.float32) * scale
                  + cb[..., None] - c_keys[:, :, None, :])
        q_pos = start + jnp.arange(Q_BLOCK, dtype=jnp.int32)
        causal = key_pos[None, :] <= q_pos[:, None]
        logits = jnp.where(causal, logits, -jnp.inf)
        probs = jax.nn.softmax(logits, axis=-1).astype(v.dtype)
        return jnp.einsum('bhqk,bkhd->bqhd', probs, v)

    out = lax.map(block, (q_blocks, c_blocks, starts))
    return out.transpose(1, 0, 2, 3, 4).reshape(bsz, seq, B_HEADS, B_HEAD_DIM)


def _fox_sample(q, k_new, v_new, logf_new, k_past, v_past, logf_past):
    scale = B_HEAD_DIM ** -0.5
    t = q.shape[1]
    p_len = k_past.shape[1]
    logf_past = logf_past.astype(jnp.float32)
    c_new = jnp.cumsum(logf_new, axis=1).transpose(0, 2, 1)
    d_past = (lax.cumsum(logf_past, axis=1, reverse=True) - logf_past).transpose(0, 2, 1)
    l_past = (jnp.einsum('bthd,bshd->bhts', q, k_past).astype(jnp.float32) * scale
              + c_new[..., :, None] + d_past[..., None, :])
    l_new = (jnp.einsum('bthd,bshd->bhts', q, k_new).astype(jnp.float32) * scale
             + c_new[..., :, None] - c_new[..., None, :])
    causal = jnp.tril(jnp.ones((t, t), dtype=bool))
    l_new = jnp.where(causal, l_new, -jnp.inf)
    probs = jax.nn.softmax(jnp.concatenate([l_past, l_new], axis=-1), axis=-1).astype(v_new.dtype)
    return (jnp.einsum('bhts,bshd->bthd', probs[..., :p_len], v_past)
            + jnp.einsum('bhts,bshd->bthd', probs[..., p_len:], v_new))


def _finish(x, a, za, b, zb, w_out, ln_post_g, p, w_pe, w_pg, b_pg):
    bsz, t = x.shape[:2]
    mix = jnp.concatenate([a.reshape(bsz, t, A_WIDTH) * jax.nn.silu(za),
                           b.reshape(bsz, t, B_WIDTH) * jax.nn.silu(zb)], axis=-1)
    x = x + _rmsnorm(mix @ w_out, ln_post_g)
    gate = jax.nn.sigmoid((x @ w_pg + b_pg).astype(jnp.float32)).astype(x.dtype)
    return x + gate * (p.astype(x.dtype) @ w_pe)


def setup_inputs(seed: int = 0) -> dict:
    key = jax.random.key(seed)
    ks = jax.random.split(key, 24)
    f32 = jnp.float32
    n_pages = PAST_LEN // PAGE_SIZE
    n_used = DEC_BATCH * n_pages
    n_pool = (5 * n_used + 3) // 4
    page_table = jax.random.permutation(ks[0], n_pool)[:n_used].reshape(DEC_BATCH, n_pages).astype(jnp.int32)
    nrm = lambda k, shape, s=1.0: jax.random.normal(k, shape, f32) * s
    return {
        "x_prompt": nrm(ks[1], (BATCH, SEQ, D_MODEL)),
        "x_sample": nrm(ks[2], (DEC_BATCH, DEC_SEQ, D_MODEL)),
        "cache_k": nrm(ks[3], (DEPTH, n_pool, PAGE_SIZE, B_HEADS, B_HEAD_DIM)),
        "cache_v": nrm(ks[4], (DEPTH, n_pool, PAGE_SIZE, B_HEADS, B_HEAD_DIM)),
        "cache_logf": jax.nn.log_sigmoid(FORGET_BIAS_INIT + nrm(ks[5], (DEPTH, n_pool, PAGE_SIZE, B_HEADS))),
        "page_table": page_table,
        "p_prompt": nrm(ks[6], (DEPTH, BATCH, SEQ, PLE_DIM)),
        "p_sample": nrm(ks[7], (DEPTH, DEC_BATCH, DEC_SEQ, PLE_DIM)),
        "ln_pre_g": 1.0 + nrm(ks[8], (DEPTH, D_MODEL), 0.05),
        "w_in": nrm(ks[9], (DEPTH, D_MODEL, IN_WIDTH), D_MODEL ** -0.5),
        "b_f": FORGET_BIAS_INIT + nrm(ks[10], (DEPTH, B_HEADS), 0.1),
        "ln_v_g": 1.0 + nrm(ks[11], (DEPTH, A_GROUPS, A_GROUP_DIM), 0.05),
        "ln_v_b": nrm(ks[12], (DEPTH, A_GROUPS, A_GROUP_DIM), 0.01),
        "w_s": nrm(ks[13], (DEPTH, A_GROUPS, CHUNK, CHUNK), CHUNK ** -0.5),
        "b_s": 1.0 + nrm(ks[14], (DEPTH, A_GROUPS, CHUNK), 0.05),
        "w_out": nrm(ks[15], (DEPTH, MIX_WIDTH, D_MODEL), MIX_WIDTH ** -0.5),
        "ln_post_g": 1.0 + nrm(ks[16], (DEPTH, D_MODEL), 0.05),
        "w_pe": nrm(ks[17], (DEPTH, PLE_DIM, D_MODEL), PLE_DIM ** -0.5),
        "w_pg": nrm(ks[18], (DEPTH, D_MODEL, D_MODEL), D_MODEL ** -0.5),
        "b_pg": nrm(ks[19], (DEPTH, D_MODEL), 0.01),
    }


def reference(x_prompt, x_sample, cache_k, cache_v, cache_logf, page_table, p_prompt, p_sample,
              ln_pre_g, w_in, b_f, ln_v_g, ln_v_b, w_s, b_s, w_out, ln_post_g, w_pe, w_pg, b_pg):
    n_pages = page_table.shape[1]
    past = n_pages * PAGE_SIZE
    xp, xs = x_prompt, x_sample
    kp_l, vp_l, fp_l, ks_l, vs_l, fs_l, cs_l = [], [], [], [], [], [], []
    for i in range(DEPTH):
        h = _rmsnorm(xp, ln_pre_g[i])
        ua, va, za, q, k, vb, zb, logf = _split_projection(h, w_in[i], b_f[i])
        vn = _layernorm(va, ln_v_g[i], ln_v_b[i])
        a = ua * _chunk_mix_prompt(vn, w_s[i], b_s[i])
        b = _fox_prompt(q, k, vb, logf)
        xp = _finish(xp, a, za, b, zb, w_out[i], ln_post_g[i], p_prompt[i], w_pe[i], w_pg[i], b_pg[i])
        kp_l.append(k); vp_l.append(vb); fp_l.append(logf)

        h = _rmsnorm(xs, ln_pre_g[i])
        ua, va, za, q, k, vb, zb, logf = _split_projection(h, w_in[i], b_f[i])
        vn = _layernorm(va, ln_v_g[i], ln_v_b[i])
        a = ua * _chunk_mix_sample(vn, w_s[i], b_s[i])
        k_past = cache_k[i][page_table].reshape(DEC_BATCH, past, B_HEADS, B_HEAD_DIM)
        v_past = cache_v[i][page_table].reshape(DEC_BATCH, past, B_HEADS, B_HEAD_DIM)
        f_past = cache_logf[i][page_table].reshape(DEC_BATCH, past, B_HEADS)
        b = _fox_sample(q, k, vb, logf, k_past, v_past, f_past)
        xs = _finish(xs, a, za, b, zb, w_out[i], ln_post_g[i], p_sample[i], w_pe[i], w_pg[i], b_pg[i])
        ks_l.append(k); vs_l.append(vb); fs_l.append(logf); cs_l.append(vn)

    new_k_prompt = jnp.stack(kp_l, 0)
    new_v_prompt = jnp.stack(vp_l, 0)
    new_logf_prompt = jnp.stack(fp_l, 0)
    new_k_sample = jnp.stack(ks_l, 0)
    new_v_sample = jnp.stack(vs_l, 0)
    new_logf_sample = jnp.stack(fs_l, 0)
    new_chunk_v_sample = jnp.stack(cs_l, 0)
    return (xp, xs, new_k_prompt, new_v_prompt, new_logf_prompt,
            new_k_sample, new_v_sample, new_logf_sample, new_chunk_v_sample)
```

```python
import functools
import math

import jax
import jax.numpy as jnp
from jax import lax
from jax.experimental import pallas as pl
from jax.experimental.pallas import tpu as pltpu

D_MODEL = 1024
A_WIDTH = 512
B_WIDTH = 512
CHUNK = 128
A_GROUPS = 4
B_HEADS = 8
B_HEAD_DIM = 64
PLE_DIM = 256
PAGE_SIZE = 128
RMS_EPS = 1e-6
LN_EPS = 1e-5
MAIN_WIDTH = 3 * A_WIDTH + 4 * B_WIDTH
LANES = 128
LOG2E = math.log2(math.e)
QK_SCALE = B_HEAD_DIM ** -0.5
NEG_BIG = -1e30

PROJ_ROWS = 256
ATTN_TILE = 512
PAGES_PER_STEP = 8
VMEM_LIMIT = 56 * 1024 * 1024

f32 = jnp.float32
bf16 = jnp.bfloat16


def _log_sigmoid(x):
    return jnp.minimum(x, 0.0) - jnp.log1p(jnp.exp(-jnp.abs(x)))


def _split3_bf16(x):
    hi = x.astype(bf16).astype(f32)
    r = x - hi
    mid = r.astype(bf16).astype(f32)
    lo = (r - mid).astype(bf16).astype(f32)
    return hi, mid, lo


def _bias_columns(nb):
    hi, mid, lo = _split3_bf16(nb)
    packed = hi + pltpu.roll(mid, B_HEADS, axis=1) + pltpu.roll(lo, 2 * B_HEADS, axis=1)
    return packed.astype(bf16)


def _project(x, g_ref, w_ref, wf_ref, bf_ref, lng_ref, lnb_ref):
    ms = jnp.mean(x * x, axis=-1, keepdims=True)
    h = (x * lax.rsqrt(ms + RMS_EPS) * g_ref[...]).astype(bf16)

    def sec(i):
        return jnp.dot(h, w_ref[:, i * 512:(i + 1) * 512], preferred_element_type=f32)

    ua = jax.nn.gelu(sec(0))
    va = jax.nn.gelu(sec(1))
    za = sec(2)
    q = sec(3)
    k = sec(4)
    v = sec(5)
    zb = sec(6)
    f = jnp.dot(h, wf_ref[...], preferred_element_type=f32) + bf_ref[...]
    col = lax.broadcasted_iota(jnp.int32, f.shape, 1)
    logf = jnp.where(col < B_HEADS, _log_sigmoid(f), 0.0)
    vn = []
    for g in range(A_GROUPS):
        vg = va[:, g * CHUNK:(g + 1) * CHUNK]
        mu = jnp.mean(vg, axis=-1, keepdims=True)
        var = jnp.mean(jnp.square(vg - mu), axis=-1, keepdims=True)
        vn.append((vg - mu) * lax.rsqrt(var + LN_EPS) * lng_ref[:, g * CHUNK:(g + 1) * CHUNK]
                  + lnb_ref[:, g * CHUNK:(g + 1) * CHUNK])
    return ua, vn, za, q, k, v, zb, logf


def _lane_cumsum(x):
    n = x.shape[-1]
    idx = lax.broadcasted_iota(jnp.int32, x.shape, x.ndim - 1)
    sh = 1
    while sh < n:
        x = x + jnp.where(idx >= sh, pltpu.roll(x, sh, axis=x.ndim - 1), 0.0)
        sh *= 2
    return x


def _lane_suffix_sum(x):
    n = x.shape[-1]
    idx = lax.broadcasted_iota(jnp.int32, x.shape, x.ndim - 1)
    sh = 1
    while sh < n:
        x = x + jnp.where(idx < n - sh, pltpu.roll(x, n - sh, axis=x.ndim - 1), 0.0)
        sh *= 2
    return x


def _proj_prompt_kernel(x_ref, g_ref, w_ref, wf_ref, bf_ref, lng_ref, lnb_ref, ws_ref, bsT_ref,
                        kT_ref, vT_ref, lfT_ref, qT16_ref, k16_ref, kb16_ref, vT16_ref, ga_ref, gzb_ref,
                        carry_ref):
    rows = x_ref.shape[1]

    @pl.when(pl.program_id(1) == 0)
    def _():
        carry_ref[...] = jnp.zeros_like(carry_ref)

    ua, vn, za, q, k, v, zb, logf = _project(x_ref[0], g_ref, w_ref, wf_ref, bf_ref, lng_ref, lnb_ref)

    tri = (lax.broadcasted_iota(jnp.int32, (CHUNK, CHUNK), 0)
           >= lax.broadcasted_iota(jnp.int32, (CHUNK, CHUNK), 1))
    mixed = []
    for g in range(A_GROUPS):
        wm = jnp.where(tri, ws_ref[g], 0.0).astype(bf16)
        bcol = jnp.broadcast_to(bsT_ref[:, g:g + 1], (CHUNK, CHUNK))
        vg = vn[g].astype(bf16)
        parts = [jnp.dot(wm, vg[c * CHUNK:(c + 1) * CHUNK], preferred_element_type=f32) + bcol
                 for c in range(rows // CHUNK)]
        mixed.append(jnp.concatenate(parts, axis=0))
    mix = jnp.concatenate(mixed, axis=1)
    ga_ref[0] = (ua * mix * jax.nn.silu(za)).astype(bf16)
    gzb_ref[0] = jax.nn.silu(zb)

    kT = k.T
    vT = v.T
    kT_ref[0] = kT
    vT_ref[0] = vT
    vT16_ref[0] = vT.astype(bf16)
    k16_ref[0] = k.astype(bf16)
    qT16_ref[0] = (q * (QK_SCALE * LOG2E)).T.astype(bf16)

    lfT = logf.T[0:B_HEADS, :]
    lfT_ref[0] = lfT
    cT = _lane_cumsum(lfT) + carry_ref[:, 0:1]
    carry_ref[...] = jnp.broadcast_to(cT[:, rows - 1:rows], carry_ref.shape)
    c_rows = jnp.concatenate([cT, jnp.zeros((LANES - B_HEADS, rows), f32)], axis=0).T
    kb16_ref[0] = _bias_columns(c_rows * (-LOG2E))


def _proj_prompt(x, g, w16, wf16, bfp, lng, lnb, ws, bsT):
    bsz, seq, _ = x.shape
    rows = PROJ_ROWS
    const = lambda *shape: pl.BlockSpec(shape, lambda b, i: (0,) * len(shape))
    tile_rows = lambda width: pl.BlockSpec((1, rows, width), lambda b, i: (b, i, 0))
    tile_cols = lambda height: pl.BlockSpec((1, height, rows), lambda b, i: (b, 0, i))
    out_shape = (
        jax.ShapeDtypeStruct((bsz, B_WIDTH, seq), f32),
        jax.ShapeDtypeStruct((bsz, B_WIDTH, seq), f32),
        jax.ShapeDtypeStruct((bsz, B_HEADS, seq), f32),
        jax.ShapeDtypeStruct((bsz, B_WIDTH, seq), bf16),
        jax.ShapeDtypeStruct((bsz, seq, B_WIDTH), bf16),
        jax.ShapeDtypeStruct((bsz, seq, LANES), bf16),
        jax.ShapeDtypeStruct((bsz, B_WIDTH, seq), bf16),
        jax.ShapeDtypeStruct((bsz, seq, A_WIDTH), bf16),
        jax.ShapeDtypeStruct((bsz, seq, B_WIDTH), f32),
    )
    out_specs = (tile_cols(B_WIDTH), tile_cols(B_WIDTH), tile_cols(B_HEADS), tile_cols(B_WIDTH),
                 tile_rows(B_WIDTH), tile_rows(LANES), tile_cols(B_WIDTH), tile_rows(A_WIDTH), tile_rows(B_WIDTH))
    return pl.pallas_call(
        _proj_prompt_kernel,
        grid=(bsz, seq // rows),
        in_specs=[tile_rows(D_MODEL), const(1, D_MODEL), const(D_MODEL, MAIN_WIDTH), const(D_MODEL, LANES),
                  const(1, LANES), const(1, A_WIDTH), const(1, A_WIDTH), const(A_GROUPS, CHUNK, CHUNK),
                  const(CHUNK, A_GROUPS)],
        out_specs=out_specs,
        out_shape=out_shape,
        scratch_shapes=[pltpu.VMEM((B_HEADS, LANES), f32)],
        compiler_params=pltpu.CompilerParams(dimension_semantics=("arbitrary", "arbitrary"),
                                             vmem_limit_bytes=VMEM_LIMIT),
        name="proj_prompt",
    )(x, g, w16, wf16, bfp, lng, lnb, ws, bsT)


def _flash_kernel(kp_ref, kb_ref, qT_ref, vT_ref, o_ref, qaug_ref, acc_ref, m_ref, l_ref):
    h = pl.program_id(1)
    par = h % 2
    seq = kp_ref.shape[1]
    t = ATTN_TILE
    row = lax.broadcasted_iota(jnp.int32, (LANES, t), 0)
    ones_rows = jnp.where((row == h) | (row == h + B_HEADS) | (row == h + 2 * B_HEADS), 1.0, 0.0).astype(bf16)
    kpos = lax.broadcasted_iota(jnp.int32, (t, t), 0)
    qpos = lax.broadcasted_iota(jnp.int32, (t, t), 1)

    def tile(k0, masked):
        kt = jnp.concatenate([kp_ref[0, pl.ds(k0, t), :], kb_ref[0, pl.ds(k0, t), :]], axis=1)
        s = jnp.dot(kt, qaug_ref[...], preferred_element_type=f32)
        if masked:
            s = jnp.where(kpos <= qpos, s, NEG_BIG)
        m_old = m_ref[...]
        m_new = jnp.maximum(m_old, jnp.max(s, axis=0, keepdims=True))
        p = jnp.exp2(s - m_new)
        alpha = jnp.exp2(m_old - m_new)
        l_ref[...] = alpha * l_ref[...] + jnp.sum(p, axis=0, keepdims=True)
        acc_ref[...] = alpha * acc_ref[...] + jnp.dot(vT_ref[0, :, pl.ds(k0, t)], p.astype(bf16),
                                                      preferred_element_type=f32)
        m_ref[...] = m_new

    def q_body(qi, carry):
        q0 = pl.multiple_of(qi * t, t)
        qp = qT_ref[0, :, pl.ds(q0, t)].astype(f32)
        qaug_ref[0:LANES, :] = jnp.where((row // B_HEAD_DIM) == par, qp, 0.0).astype(bf16)
        qaug_ref[LANES:2 * LANES, :] = ones_rows
        m_ref[...] = jnp.full_like(m_ref, NEG_BIG)
        l_ref[...] = jnp.zeros_like(l_ref)
        acc_ref[...] = jnp.zeros_like(acc_ref)

        def kv_body(kj, c):
            tile(pl.multiple_of(kj * t, t), False)
            return c

        lax.fori_loop(0, qi, kv_body, 0)
        tile(q0, True)
        o_ref[0, :, pl.ds(q0, t)] = acc_ref[...] / l_ref[...]
        return carry

    lax.fori_loop(0, seq // t, q_body, 0)


def _flash_prompt(k16, kb16, qT16, vT16):
    bsz, seq, _ = k16.shape
    t = ATTN_TILE
    return pl.pallas_call(
        _flash_kernel,
        grid=(bsz, B_HEADS),
        in_specs=[pl.BlockSpec((1, seq, LANES), lambda b, h: (b, 0, h // 2)),
                  pl.BlockSpec((1, seq, LANES), lambda b, h: (b, 0, 0)),
                  pl.BlockSpec((1, LANES, seq), lambda b, h: (b, h // 2, 0)),
                  pl.BlockSpec((1, B_HEAD_DIM, seq), lambda b, h: (b, h, 0))],
        out_specs=pl.BlockSpec((1, B_HEAD_DIM, seq), lambda b, h: (b, h, 0)),
        out_shape=jax.ShapeDtypeStruct((bsz, B_WIDTH, seq), f32),
        scratch_shapes=[pltpu.VMEM((2 * LANES, t), bf16), pltpu.VMEM((B_HEAD_DIM, t), f32),
                        pltpu.VMEM((1, t), f32), pltpu.VMEM((1, t), f32)],
        compiler_params=pltpu.CompilerParams(dimension_semantics=("arbitrary", "arbitrary"),
                                             vmem_limit_bytes=VMEM_LIMIT),
        name="flash_prompt",
    )(k16, kb16, qT16, vT16)


def _proj_sample_kernel(x_ref, g_ref, w_ref, wf_ref, bf_ref, lng_ref, lnb_ref, ws_ref, bsT_ref,
                        k_ref, v_ref, lf_ref, vn_ref, q_ref, kaug_ref, v16_ref, ga_ref, gzb_ref):
    rows = x_ref.shape[0]
    t = 8
    ua, vn, za, q, k, v, zb, logf = _project(x_ref[...], g_ref, w_ref, wf_ref, bf_ref, lng_ref, lnb_ref)

    r_i = lax.broadcasted_iota(jnp.int32, (rows, rows), 0)
    c_i = lax.broadcasted_iota(jnp.int32, (rows, rows), 1)
    block = ((r_i // t) == (c_i // t)) & ((c_i % t) <= (r_i % t))
    spread = (lax.broadcasted_iota(jnp.int32, (LANES, rows), 0)
              == lax.broadcasted_iota(jnp.int32, (LANES, rows), 1) % t)
    spread = jnp.where(spread, 1.0, 0.0).astype(bf16)
    mixed = []
    for g in range(A_GROUPS):
        w8 = jnp.dot(ws_ref[g, 0:t, :].astype(bf16), spread, preferred_element_type=f32)
        wt = jnp.broadcast_to(w8[None], (rows // t, t, rows)).reshape(rows, rows)
        wbd = jnp.where(block, wt, 0.0).astype(bf16)
        b8 = jnp.broadcast_to(bsT_ref[0:t, g:g + 1], (t, CHUNK))
        bcol = jnp.broadcast_to(b8[None], (rows // t, t, CHUNK)).reshape(rows, CHUNK)
        mixed.append(jnp.dot(wbd, vn[g].astype(bf16), preferred_element_type=f32) + bcol)
    mix = jnp.concatenate(mixed, axis=1)
    ga_ref[...] = (ua * mix * jax.nn.silu(za)).astype(bf16)
    gzb_ref[...] = jax.nn.silu(zb)
    vn_ref[...] = jnp.concatenate(vn, axis=1)
    k_ref[...] = k
    v_ref[...] = v
    v16_ref[...] = v.astype(bf16)
    lf_ref[...] = logf
    q_ref[...] = q * (QK_SCALE * LOG2E)

    lf3 = logf.reshape(rows // t, t, LANES)
    tok = lax.broadcasted_iota(jnp.int32, lf3.shape, 1)
    c3 = jnp.zeros_like(lf3)
    for s in range(t):
        c3 = c3 + jnp.where(tok >= s, jnp.broadcast_to(lf3[:, s:s + 1, :], lf3.shape), 0.0)
    c_rows = c3.reshape(rows, LANES)
    kaug_ref[:, 0:B_WIDTH] = k.astype(bf16)
    kaug_ref[:, B_WIDTH:B_WIDTH + LANES] = _bias_columns(c_rows * (-LOG2E))


def _proj_sample(x, g, w16, wf16, bfp, lng, lnb, ws, bsT):
    n = x.shape[0]
    rows = PROJ_ROWS
    const = lambda *shape: pl.BlockSpec(shape, lambda i: (0,) * len(shape))
    tile = lambda width: pl.BlockSpec((rows, width), lambda i: (i, 0))
    out_shape = (
        jax.ShapeDtypeStruct((n, B_WIDTH), f32),
        jax.ShapeDtypeStruct((n, B_WIDTH), f32),
        jax.ShapeDtypeStruct((n, LANES), f32),
        jax.ShapeDtypeStruct((n, A_WIDTH), f32),
        jax.ShapeDtypeStruct((n, B_WIDTH), f32),
        jax.ShapeDtypeStruct((n, B_WIDTH + LANES), bf16),
        jax.ShapeDtypeStruct((n, B_WIDTH), bf16),
        jax.ShapeDtypeStruct((n, A_WIDTH), bf16),
        jax.ShapeDtypeStruct((n, B_WIDTH), f32),
    )
    out_specs = (tile(B_WIDTH), tile(B_WIDTH), tile(LANES), tile(A_WIDTH), tile(B_WIDTH),
                 tile(B_WIDTH + LANES), tile(B_WIDTH), tile(A_WIDTH), tile(B_WIDTH))
    return pl.pallas_call(
        _proj_sample_kernel,
        grid=(n // rows,),
        in_specs=[tile(D_MODEL), const(1, D_MODEL), const(D_MODEL, MAIN_WIDTH), const(D_MODEL, LANES),
                  const(1, LANES), const(1, A_WIDTH), const(1, A_WIDTH), const(A_GROUPS, CHUNK, CHUNK),
                  const(CHUNK, A_GROUPS)],
        out_specs=out_specs,
        out_shape=out_shape,
        compiler_params=pltpu.CompilerParams(dimension_semantics=("arbitrary",),
                                             vmem_limit_bytes=VMEM_LIMIT),
        name="proj_sample",
    )(x, g, w16, wf16, bfp, lng, lnb, ws, bsT)


def _decode_kernel(pt_ref, *refs):
    npg = PAGES_PER_STEP
    k_refs = refs[0:npg]
    v_refs = refs[npg:2 * npg]
    f_refs = refs[2 * npg:3 * npg]
    q_ref, kaug_ref, v16_ref, o_ref, qbd_ref, acc_ref, m_ref, l_ref, carry_ref = refs[3 * npg:]
    j = pl.program_id(1)
    t = 8
    rows = B_HEADS * t
    width = npg * PAGE_SIZE

    @pl.when(j == 0)
    def _():
        qrep = jnp.concatenate([q_ref[0]] * B_HEADS, axis=0)
        r_i = lax.broadcasted_iota(jnp.int32, (rows, B_WIDTH), 0)
        c_i = lax.broadcasted_iota(jnp.int32, (rows, B_WIDTH), 1)
        qbd_ref[:, 0:B_WIDTH] = jnp.where((r_i // t) == (c_i // B_HEAD_DIM), qrep, 0.0).astype(bf16)
        r_b = lax.broadcasted_iota(jnp.int32, (rows, LANES), 0) // t
        c_b = lax.broadcasted_iota(jnp.int32, (rows, LANES), 1)
        ones = (c_b == r_b) | (c_b == r_b + B_HEADS) | (c_b == r_b + 2 * B_HEADS)
        qbd_ref[:, B_WIDTH:B_WIDTH + LANES] = jnp.where(ones, 1.0, 0.0).astype(bf16)
        m_ref[...] = jnp.full_like(m_ref, NEG_BIG)
        l_ref[...] = jnp.zeros_like(l_ref)
        acc_ref[...] = jnp.zeros_like(acc_ref)
        carry_ref[...] = jnp.zeros_like(carry_ref)

    def update(s, vals, contract_lanes):
        m_old = m_ref[...]
        m_new = jnp.maximum(m_old, jnp.max(s, axis=1, keepdims=True))
        p = jnp.exp2(s - m_new)
        alpha = jnp.exp2(m_old - m_new)
        l_ref[...] = alpha * l_ref[...] + jnp.sum(p, axis=1, keepdims=True)
        if contract_lanes:
            pv = lax.dot_general(p.astype(bf16), vals, (((1,), (1,)), ((), ())), preferred_element_type=f32)
        else:
            pv = jnp.dot(p.astype(bf16), vals, preferred_element_type=f32)
        acc_ref[...] = alpha * acc_ref[...] + pv
        m_ref[...] = m_new

    kt = jnp.concatenate([r[0].astype(bf16) for r in k_refs], axis=1)
    vt = jnp.concatenate([r[0].astype(bf16) for r in v_refs], axis=1)
    lf = jnp.concatenate([r[0] for r in f_refs], axis=1)
    incl = _lane_suffix_sum(lf)
    d = (incl - lf + carry_ref[:, 0:1]) * LOG2E
    carry_ref[...] = carry_ref[...] + jnp.broadcast_to(incl[:, 0:1], carry_ref.shape)
    dfull = jnp.concatenate([jnp.broadcast_to(d[hh:hh + 1, :], (t, width)) for hh in range(B_HEADS)], axis=0)
    s = jnp.dot(qbd_ref[:, 0:B_WIDTH], kt, preferred_element_type=f32) + dfull
    update(s, vt, True)

    @pl.when(j == pl.num_programs(1) - 1)
    def _():
        sn = lax.dot_general(qbd_ref[...], kaug_ref[0], (((1,), (1,)), ((), ())),
                             preferred_element_type=f32)
        tq = lax.broadcasted_iota(jnp.int32, sn.shape, 0) % t
        tk = lax.broadcasted_iota(jnp.int32, sn.shape, 1)
        update(jnp.where(tk <= tq, sn, NEG_BIG), v16_ref[0], False)
        res = acc_ref[...] / l_ref[...]
        r_i = lax.broadcasted_iota(jnp.int32, (t, B_WIDTH), 1) // B_HEAD_DIM
        out = jnp.zeros((t, B_WIDTH), f32)
        for hh in range(B_HEADS):
            out = out + jnp.where(r_i == hh, res[hh * t:(hh + 1) * t, :], 0.0)
        o_ref[0] = out


def _decode_attention(page_table, kT_cache, vT_cache, lf_cache, q, kaug, v16):
    nb, n_pages = page_table.shape
    npg = PAGES_PER_STEP
    steps = n_pages // npg
    t = 8

    def page_map(i):
        return lambda b, j, pt: (pt[b, (steps - 1 - j) * npg + i], 0, 0)

    per_seq = lambda width: pl.BlockSpec((1, t, width), lambda b, j, pt: (b, 0, 0))
    in_specs = ([pl.BlockSpec((1, B_WIDTH, PAGE_SIZE), page_map(i)) for i in range(npg)]
                + [pl.BlockSpec((1, B_WIDTH, PAGE_SIZE), page_map(i)) for i in range(npg)]
                + [pl.BlockSpec((1, B_HEADS, PAGE_SIZE), page_map(i)) for i in range(npg)]
                + [per_seq(B_WIDTH), per_seq(B_WIDTH + LANES), per_seq(B_WIDTH)])
    grid_spec = pltpu.PrefetchScalarGridSpec(
        num_scalar_prefetch=1,
        grid=(nb, steps),
        in_specs=in_specs,
        out_specs=per_seq(B_WIDTH),
        scratch_shapes=[pltpu.VMEM((B_HEADS * t, B_WIDTH + LANES), bf16),
                        pltpu.VMEM((B_HEADS * t, B_WIDTH), f32),
                        pltpu.VMEM((B_HEADS * t, 1), f32), pltpu.VMEM((B_HEADS * t, 1), f32),
                        pltpu.VMEM((B_HEADS, LANES), f32)],
    )
    return pl.pallas_call(
        _decode_kernel,
        grid_spec=grid_spec,
        out_shape=jax.ShapeDtypeStruct((nb, t, B_WIDTH), f32),
        compiler_params=pltpu.CompilerParams(dimension_semantics=("arbitrary", "arbitrary"),
                                             vmem_limit_bytes=VMEM_LIMIT),
        name="decode_attention",
    )(page_table, *([kT_cache] * npg), *([vT_cache] * npg), *([lf_cache] * npg),
      q.reshape(nb, t, B_WIDTH), kaug.reshape(nb, t, B_WIDTH + LANES), v16.reshape(nb, t, B_WIDTH))


def _finish_kernel(transposed, x_ref, ga_ref, o_ref, gzb_ref, p_ref, wo_ref, gpost_ref, wpg_ref, bpg_ref,
                   wpe_ref, y_ref):
    batched = x_ref.ndim == 3
    ld = (lambda r: r[0]) if batched else (lambda r: r[...])
    x = ld(x_ref)
    o = ld(o_ref)
    if transposed:
        o = o.T
    mix_b = (o * ld(gzb_ref)).astype(bf16)
    y = (jnp.dot(ld(ga_ref), wo_ref[0:A_WIDTH, :], preferred_element_type=f32)
         + jnp.dot(mix_b, wo_ref[A_WIDTH:, :], preferred_element_type=f32))
    ms = jnp.mean(y * y, axis=-1, keepdims=True)
    x = x + y * lax.rsqrt(ms + RMS_EPS) * gpost_ref[...]
    gate = jax.nn.sigmoid(jnp.dot(x.astype(bf16), wpg_ref[...], preferred_element_type=f32) + bpg_ref[...])
    res = x + gate * jnp.dot(ld(p_ref).astype(bf16), wpe_ref[...], preferred_element_type=f32)
    if batched:
        y_ref[0] = res
    else:
        y_ref[...] = res


def _finish_prompt(x, ga, oT, gzb, p, wo16, gpost, wpg16, bpg, wpe16):
    bsz, seq, _ = x.shape
    rows = PROJ_ROWS
    const = lambda *shape: pl.BlockSpec(shape, lambda b, i: (0,) * len(shape))
    tile_rows = lambda width: pl.BlockSpec((1, rows, width), lambda b, i: (b, i, 0))
    return pl.pallas_call(
        functools.partial(_finish_kernel, True),
        grid=(bsz, seq // rows),
        in_specs=[tile_rows(D_MODEL), tile_rows(A_WIDTH),
                  pl.BlockSpec((1, B_WIDTH, rows), lambda b, i: (b, 0, i)),
                  tile_rows(B_WIDTH), tile_rows(PLE_DIM), const(D_MODEL, D_MODEL), const(1, D_MODEL),
                  const(D_MODEL, D_MODEL), const(1, D_MODEL), const(PLE_DIM, D_MODEL)],
        out_specs=tile_rows(D_MODEL),
        out_shape=jax.ShapeDtypeStruct(x.shape, f32),
        compiler_params=pltpu.CompilerParams(dimension_semantics=("arbitrary", "arbitrary"),
                                             vmem_limit_bytes=VMEM_LIMIT),
        name="finish_prompt",
    )(x, ga, oT, gzb, p, wo16, gpost, wpg16, bpg, wpe16)


def _finish_sample(x, ga, o, gzb, p, wo16, gpost, wpg16, bpg, wpe16):
    n = x.shape[0]
    rows = PROJ_ROWS
    const = lambda *shape: pl.BlockSpec(shape, lambda i: (0,) * len(shape))
    tile = lambda width: pl.BlockSpec((rows, width), lambda i: (i, 0))
    return pl.pallas_call(
        functools.partial(_finish_kernel, False),
        grid=(n // rows,),
        in_specs=[tile(D_MODEL), tile(A_WIDTH), tile(B_WIDTH), tile(B_WIDTH), tile(PLE_DIM),
                  const(D_MODEL, D_MODEL), const(1, D_MODEL), const(D_MODEL, D_MODEL), const(1, D_MODEL),
                  const(PLE_DIM, D_MODEL)],
        out_specs=tile(D_MODEL),
        out_shape=jax.ShapeDtypeStruct(x.shape, f32),
        compiler_params=pltpu.CompilerParams(dimension_semantics=("arbitrary",),
                                             vmem_limit_bytes=VMEM_LIMIT),
        name="finish_sample",
    )(x, ga, o, gzb, p, wo16, gpost, wpg16, bpg, wpe16)


def kernel(x_prompt, x_sample, cache_k, cache_v, cache_logf, page_table, p_prompt, p_sample, ln_pre_g, w_in, b_f,
           ln_v_g, ln_v_b, w_s, b_s, w_out, ln_post_g, w_pe, w_pg, b_pg):
    assert w_in.shape[0] == 1, "single-layer step"
    bsz, seq, _ = x_prompt.shape
    nb, t, _ = x_sample.shape
    n_pool = cache_k.shape[1]

    w16 = w_in[0, :, :MAIN_WIDTH].astype(bf16)
    wf16 = jnp.pad(w_in[0, :, MAIN_WIDTH:], ((0, 0), (0, LANES - B_HEADS))).astype(bf16)
    bfp = jnp.pad(b_f[0], (0, LANES - B_HEADS)).reshape(1, LANES)
    g_pre = ln_pre_g[0].reshape(1, D_MODEL)
    lng = ln_v_g[0].reshape(1, A_WIDTH)
    lnb = ln_v_b[0].reshape(1, A_WIDTH)
    bsT = b_s[0].T
    wo16 = w_out[0].astype(bf16)
    wpg16 = w_pg[0].astype(bf16)
    wpe16 = w_pe[0].astype(bf16)
    gpost = ln_post_g[0].reshape(1, D_MODEL)
    bpg = b_pg[0].reshape(1, D_MODEL)

    kT, vT, lfT, qT16, k16, kb16, vT16, ga, gzb = _proj_prompt(x_prompt, g_pre, w16, wf16, bfp, lng, lnb, w_s[0], bsT)
    oT = _flash_prompt(k16, kb16, qT16, vT16)
    y_prompt = _finish_prompt(x_prompt, ga, oT, gzb, p_prompt[0], wo16, gpost, wpg16, bpg, wpe16)
    new_k_prompt = kT.reshape(bsz, B_HEADS, B_HEAD_DIM, seq).transpose(0, 3, 1, 2)[None]
    new_v_prompt = vT.reshape(bsz, B_HEADS, B_HEAD_DIM, seq).transpose(0, 3, 1, 2)[None]
    new_logf_prompt = lfT.transpose(0, 2, 1)[None]

    kT_cache = cache_k[0].transpose(0, 2, 3, 1).reshape(n_pool, B_WIDTH, PAGE_SIZE)
    vT_cache = cache_v[0].transpose(0, 2, 3, 1).reshape(n_pool, B_WIDTH, PAGE_SIZE)
    lf_cache = cache_logf[0].transpose(0, 2, 1)
    xs = x_sample.reshape(nb * t, D_MODEL)
    ks, vs, lfs, vns, qs, kaug, vs16, gas, gzbs = _proj_sample(xs, g_pre, w16, wf16, bfp, lng, lnb, w_s[0], bsT)
    o_s = _decode_attention(page_table, kT_cache, vT_cache, lf_cache, qs, kaug, vs16)
    y_sample = _finish_sample(xs, gas, o_s.reshape(nb * t, B_WIDTH), gzbs, p_sample[0].reshape(nb * t, PLE_DIM),
                              wo16, gpost, wpg16, bpg, wpe16).reshape(nb, t, D_MODEL)

    return (y_prompt, y_sample, new_k_prompt, new_v_prompt, new_logf_prompt,
            ks.reshape(1, nb, t, B_HEADS, B_HEAD_DIM), vs.reshape(1, nb, t, B_HEADS, B_HEAD_DIM),
            lfs[:, :B_HEADS].reshape(1, nb, t, B_HEADS), vns.reshape(1, nb, t, A_GROUPS, CHUNK))
```

```python
import functools
import math

import jax
import jax.numpy as jnp
from jax import lax
from jax.experimental import pallas as pl
from jax.experimental.pallas import tpu as pltpu

D_MODEL = 1024
A_WIDTH = 512
B_WIDTH = 512
CHUNK = 128
A_GROUPS = 4
B_HEADS = 8
B_HEAD_DIM = 64
PLE_DIM = 256
PAGE_SIZE = 128
RMS_EPS = 1e-6
LN_EPS = 1e-5
MAIN_WIDTH = 3 * A_WIDTH + 4 * B_WIDTH
LANES = 128
LOG2E = math.log2(math.e)
QK_SCALE = B_HEAD_DIM ** -0.5
NEG_BIG = -1e30

PROJ_ROWS = 256
Q_TILE = 1024
K_SUB = 256
PAGES_PER_STEP = 16
VMEM_LIMIT = 56 * 1024 * 1024

f32 = jnp.float32
bf16 = jnp.bfloat16


def _log_sigmoid(x):
    return jnp.minimum(x, 0.0) - jnp.log1p(jnp.exp(-jnp.abs(x)))


def _split3_bf16(x):
    hi = x.astype(bf16).astype(f32)
    r = x - hi
    mid = r.astype(bf16).astype(f32)
    lo = (r - mid).astype(bf16).astype(f32)
    return hi, mid, lo


def _bias_columns(nb):
    hi, mid, lo = _split3_bf16(nb)
    packed = hi + pltpu.roll(mid, B_HEADS, axis=1) + pltpu.roll(lo, 2 * B_HEADS, axis=1)
    return packed.astype(bf16)


def _project(x, g_ref, w_ref, wf_ref, bf_ref, lng_ref, lnb_ref):
    ms = jnp.mean(x * x, axis=-1, keepdims=True)
    h = (x * lax.rsqrt(ms + RMS_EPS) * g_ref[...]).astype(bf16)

    def sec(i):
        return jnp.dot(h, w_ref[:, i * 512:(i + 1) * 512], preferred_element_type=f32)

    ua = jax.nn.gelu(sec(0))
    va = jax.nn.gelu(sec(1))
    za = sec(2)
    q = sec(3)
    k = sec(4)
    v = sec(5)
    zb = sec(6)
    f = jnp.dot(h, wf_ref[...], preferred_element_type=f32) + bf_ref[...]
    col = lax.broadcasted_iota(jnp.int32, f.shape, 1)
    logf = jnp.where(col < B_HEADS, _log_sigmoid(f), 0.0)
    vn = []
    for g in range(A_GROUPS):
        vg = va[:, g * CHUNK:(g + 1) * CHUNK]
        mu = jnp.mean(vg, axis=-1, keepdims=True)
        var = jnp.mean(jnp.square(vg - mu), axis=-1, keepdims=True)
        vn.append((vg - mu) * lax.rsqrt(var + LN_EPS) * lng_ref[:, g * CHUNK:(g + 1) * CHUNK]
                  + lnb_ref[:, g * CHUNK:(g + 1) * CHUNK])
    return ua, vn, za, q, k, v, zb, logf


def _lane_cumsum(x):
    n = x.shape[-1]
    idx = lax.broadcasted_iota(jnp.int32, x.shape, x.ndim - 1)
    sh = 1
    while sh < n:
        x = x + jnp.where(idx >= sh, pltpu.roll(x, sh, axis=x.ndim - 1), 0.0)
        sh *= 2
    return x


def _lane_suffix_sum(x):
    n = x.shape[-1]
    idx = lax.broadcasted_iota(jnp.int32, x.shape, x.ndim - 1)
    sh = 1
    while sh < n:
        x = x + jnp.where(idx < n - sh, pltpu.roll(x, n - sh, axis=x.ndim - 1), 0.0)
        sh *= 2
    return x


def _proj_prompt_kernel(x_ref, g_ref, w_ref, wf_ref, bf_ref, lng_ref, lnb_ref, ws_ref, bsT_ref,
                        kT_ref, vT_ref, lfT_ref, qT16_ref, k16_ref, kb16_ref, vT16_ref, ga_ref, gzb_ref,
                        carry_ref):
    rows = x_ref.shape[1]

    @pl.when(pl.program_id(1) == 0)
    def _():
        carry_ref[...] = jnp.zeros_like(carry_ref)

    ua, vn, za, q, k, v, zb, logf = _project(x_ref[0], g_ref, w_ref, wf_ref, bf_ref, lng_ref, lnb_ref)

    tri = (lax.broadcasted_iota(jnp.int32, (CHUNK, CHUNK), 0)
           >= lax.broadcasted_iota(jnp.int32, (CHUNK, CHUNK), 1))
    mixed = []
    for g in range(A_GROUPS):
        wm = jnp.where(tri, ws_ref[g], 0.0).astype(bf16)
        bcol = jnp.broadcast_to(bsT_ref[:, g:g + 1], (CHUNK, CHUNK))
        vg = vn[g].astype(bf16)
        parts = [jnp.dot(wm, vg[c * CHUNK:(c + 1) * CHUNK], preferred_element_type=f32) + bcol
                 for c in range(rows // CHUNK)]
        mixed.append(jnp.concatenate(parts, axis=0))
    mix = jnp.concatenate(mixed, axis=1)
    ga_ref[0] = (ua * mix * jax.nn.silu(za)).astype(bf16)
    gzb_ref[0] = jax.nn.silu(zb)

    kT = k.T
    vT = v.T
    kT_ref[0] = kT
    vT_ref[0] = vT
    vT16_ref[0] = vT.astype(bf16)
    k16_ref[0] = k.astype(bf16)
    qT16_ref[0] = (q * (QK_SCALE * LOG2E)).T.astype(bf16)

    lfT = logf.T[0:B_HEADS, :]
    lfT_ref[0] = lfT
    cT = _lane_cumsum(lfT) + carry_ref[:, 0:1]
    carry_ref[...] = jnp.broadcast_to(cT[:, rows - 1:rows], carry_ref.shape)
    c_rows = jnp.concatenate([cT, jnp.zeros((LANES - B_HEADS, rows), f32)], axis=0).T
    kb16_ref[0] = _bias_columns(c_rows * (-LOG2E))


def _proj_prompt(x, g, w16, wf16, bfp, lng, lnb, ws, bsT):
    bsz, seq, _ = x.shape
    rows = PROJ_ROWS
    const = lambda *shape: pl.BlockSpec(shape, lambda b, i: (0,) * len(shape))
    tile_rows = lambda width: pl.BlockSpec((1, rows, width), lambda b, i: (b, i, 0))
    tile_cols = lambda height: pl.BlockSpec((1, height, rows), lambda b, i: (b, 0, i))
    out_shape = (
        jax.ShapeDtypeStruct((bsz, B_WIDTH, seq), f32),
        jax.ShapeDtypeStruct((bsz, B_WIDTH, seq), f32),
        jax.ShapeDtypeStruct((bsz, B_HEADS, seq), f32),
        jax.ShapeDtypeStruct((bsz, B_WIDTH, seq), bf16),
        jax.ShapeDtypeStruct((bsz, seq, B_WIDTH), bf16),
        jax.ShapeDtypeStruct((bsz, seq, LANES), bf16),
        jax.ShapeDtypeStruct((bsz, B_WIDTH, seq), bf16),
        jax.ShapeDtypeStruct((bsz, seq, A_WIDTH), bf16),
        jax.ShapeDtypeStruct((bsz, seq, B_WIDTH), f32),
    )
    out_specs = (tile_cols(B_WIDTH), tile_cols(B_WIDTH), tile_cols(B_HEADS), tile_cols(B_WIDTH),
                 tile_rows(B_WIDTH), tile_rows(LANES), tile_cols(B_WIDTH), tile_rows(A_WIDTH), tile_rows(B_WIDTH))
    return pl.pallas_call(
        _proj_prompt_kernel,
        grid=(bsz, seq // rows),
        in_specs=[tile_rows(D_MODEL), const(1, D_MODEL), const(D_MODEL, MAIN_WIDTH), const(D_MODEL, LANES),
                  const(1, LANES), const(1, A_WIDTH), const(1, A_WIDTH), const(A_GROUPS, CHUNK, CHUNK),
                  const(CHUNK, A_GROUPS)],
        out_specs=out_specs,
        out_shape=out_shape,
        scratch_shapes=[pltpu.VMEM((B_HEADS, LANES), f32)],
        compiler_params=pltpu.CompilerParams(dimension_semantics=("arbitrary", "arbitrary"),
                                             vmem_limit_bytes=VMEM_LIMIT),
        name="proj_prompt",
    )(x, g, w16, wf16, bfp, lng, lnb, ws, bsT)


def _flash_kernel(kp_ref, kb_ref, qT_ref, vT_ref, o_ref, qaug_ref, acc_ref, m_ref, l_ref):
    h = pl.program_id(1)
    par = h % 2
    seq = kp_ref.shape[1]
    t = Q_TILE
    ks = K_SUB
    row = lax.broadcasted_iota(jnp.int32, (LANES, t), 0)
    ones_rows = jnp.where((row == h) | (row == h + B_HEADS) | (row == h + 2 * B_HEADS), 1.0, 0.0).astype(bf16)
    causal = lax.broadcasted_iota(jnp.int32, (ks, t), 0) <= lax.broadcasted_iota(jnp.int32, (ks, t), 1)

    def scores(k0, lo):
        kt = jnp.concatenate([kp_ref[0, pl.ds(k0, ks), :], kb_ref[0, pl.ds(k0, ks), :]], axis=1)
        return jnp.dot(kt, qaug_ref[:, lo:], preferred_element_type=f32)

    def consume(s, k0, lo, masked):
        if masked:
            s = jnp.where(causal[:, :t - lo], s, NEG_BIG)
        m_old = m_ref[:, lo:]
        m_new = jnp.maximum(m_old, jnp.max(s, axis=0, keepdims=True))
        p = jnp.exp2(s - m_new)
        alpha = jnp.exp2(m_old - m_new)
        l_ref[:, lo:] = alpha * l_ref[:, lo:] + jnp.sum(p, axis=0, keepdims=True)
        acc_ref[:, lo:] = alpha * acc_ref[:, lo:] + jnp.dot(vT_ref[0, :, pl.ds(k0, ks)], p.astype(bf16),
                                                          preferred_element_type=f32)
        m_ref[:, lo:] = m_new

    def q_body(qi, carry):
        q0 = pl.multiple_of(qi * t, t)
        qp = qT_ref[0, :, pl.ds(q0, t)].astype(f32)
        qaug_ref[0:LANES, :] = jnp.where((row // B_HEAD_DIM) == par, qp, 0.0).astype(bf16)
        qaug_ref[LANES:2 * LANES, :] = ones_rows
        m_ref[...] = jnp.full_like(m_ref, NEG_BIG)
        l_ref[...] = jnp.zeros_like(l_ref)
        acc_ref[...] = jnp.zeros_like(acc_ref)

        nsub = t // ks

        def kv_body(kj, c):
            base = kj * t
            s = scores(pl.multiple_of(base, ks), 0)
            for r in range(nsub):
                s_next = scores(pl.multiple_of(base + (r + 1) * ks, ks), 0) if r + 1 < nsub else None
                consume(s, pl.multiple_of(base + r * ks, ks), 0, False)
                s = s_next
            return c

        lax.fori_loop(0, qi, kv_body, 0)
        s = scores(q0, 0)
        for r in range(nsub):
            s_next = scores(pl.multiple_of(q0 + (r + 1) * ks, ks), (r + 1) * ks) if r + 1 < nsub else None
            consume(s, pl.multiple_of(q0 + r * ks, ks), r * ks, True)
            s = s_next
        o_ref[0, :, pl.ds(q0, t)] = acc_ref[...] / l_ref[...]
        return carry

    lax.fori_loop(0, seq // t, q_body, 0)


def _flash_prompt(k16, kb16, qT16, vT16):
    bsz, seq, _ = k16.shape
    t = Q_TILE
    return pl.pallas_call(
        _flash_kernel,
        grid=(bsz, B_HEADS),
        in_specs=[pl.BlockSpec((1, seq, LANES), lambda b, h: (b, 0, h // 2)),
                  pl.BlockSpec((1, seq, LANES), lambda b, h: (b, 0, 0)),
                  pl.BlockSpec((1, LANES, seq), lambda b, h: (b, h // 2, 0)),
                  pl.BlockSpec((1, B_HEAD_DIM, seq), lambda b, h: (b, h, 0))],
        out_specs=pl.BlockSpec((1, B_HEAD_DIM, seq), lambda b, h: (b, h, 0)),
        out_shape=jax.ShapeDtypeStruct((bsz, B_WIDTH, seq), f32),
        scratch_shapes=[pltpu.VMEM((2 * LANES, t), bf16), pltpu.VMEM((B_HEAD_DIM, t), f32),
                        pltpu.VMEM((1, t), f32), pltpu.VMEM((1, t), f32)],
        compiler_params=pltpu.CompilerParams(dimension_semantics=("arbitrary", "arbitrary"),
                                             vmem_limit_bytes=VMEM_LIMIT),
        name="flash_prompt",
    )(k16, kb16, qT16, vT16)


def _proj_sample_kernel(x_ref, g_ref, w_ref, wf_ref, bf_ref, lng_ref, lnb_ref, ws_ref, bsT_ref,
                        k_ref, v_ref, lf_ref, vn_ref, q_ref, kaug_ref, v16_ref, ga_ref, gzb_ref):
    rows = x_ref.shape[0]
    t = 8
    ua, vn, za, q, k, v, zb, logf = _project(x_ref[...], g_ref, w_ref, wf_ref, bf_ref, lng_ref, lnb_ref)

    r_i = lax.broadcasted_iota(jnp.int32, (rows, rows), 0)
    c_i = lax.broadcasted_iota(jnp.int32, (rows, rows), 1)
    block = ((r_i // t) == (c_i // t)) & ((c_i % t) <= (r_i % t))
    spread = (lax.broadcasted_iota(jnp.int32, (LANES, rows), 0)
              == lax.broadcasted_iota(jnp.int32, (LANES, rows), 1) % t)
    spread = jnp.where(spread, 1.0, 0.0).astype(bf16)
    mixed = []
    for g in range(A_GROUPS):
        w8 = jnp.dot(ws_ref[g, 0:t, :].astype(bf16), spread, preferred_element_type=f32)
        wt = jnp.broadcast_to(w8[None], (rows // t, t, rows)).reshape(rows, rows)
        wbd = jnp.where(block, wt, 0.0).astype(bf16)
        b8 = jnp.broadcast_to(bsT_ref[0:t, g:g + 1], (t, CHUNK))
        bcol = jnp.broadcast_to(b8[None], (rows // t, t, CHUNK)).reshape(rows, CHUNK)
        mixed.append(jnp.dot(wbd, vn[g].astype(bf16), preferred_element_type=f32) + bcol)
    mix = jnp.concatenate(mixed, axis=1)
    ga_ref[...] = (ua * mix * jax.nn.silu(za)).astype(bf16)
    gzb_ref[...] = jax.nn.silu(zb)
    vn_ref[...] = jnp.concatenate(vn, axis=1)
    k_ref[...] = k
    v_ref[...] = v
    v16_ref[...] = v.astype(bf16)
    lf_ref[...] = logf
    q_ref[...] = q * (QK_SCALE * LOG2E)

    lf3 = logf.reshape(rows // t, t, LANES)
    tok = lax.broadcasted_iota(jnp.int32, lf3.shape, 1)
    c3 = jnp.zeros_like(lf3)
    for s in range(t):
        c3 = c3 + jnp.where(tok >= s, jnp.broadcast_to(lf3[:, s:s + 1, :], lf3.shape), 0.0)
    c_rows = c3.reshape(rows, LANES)
    kaug_ref[:, 0:B_WIDTH] = k.astype(bf16)
    kaug_ref[:, B_WIDTH:B_WIDTH + LANES] = _bias_columns(c_rows * (-LOG2E))


def _proj_sample(x, g, w16, wf16, bfp, lng, lnb, ws, bsT):
    n = x.shape[0]
    rows = PROJ_ROWS
    const = lambda *shape: pl.BlockSpec(shape, lambda i: (0,) * len(shape))
    tile = lambda width: pl.BlockSpec((rows, width), lambda i: (i, 0))
    out_shape = (
        jax.ShapeDtypeStruct((n, B_WIDTH), f32),
        jax.ShapeDtypeStruct((n, B_WIDTH), f32),
        jax.ShapeDtypeStruct((n, LANES), f32),
        jax.ShapeDtypeStruct((n, A_WIDTH), f32),
        jax.ShapeDtypeStruct((n, B_WIDTH), f32),
        jax.ShapeDtypeStruct((n, B_WIDTH + LANES), bf16),
        jax.ShapeDtypeStruct((n, B_WIDTH), bf16),
        jax.ShapeDtypeStruct((n, A_WIDTH), bf16),
        jax.ShapeDtypeStruct((n, B_WIDTH), f32),
    )
    out_specs = (tile(B_WIDTH), tile(B_WIDTH), tile(LANES), tile(A_WIDTH), tile(B_WIDTH),
                 tile(B_WIDTH + LANES), tile(B_WIDTH), tile(A_WIDTH), tile(B_WIDTH))
    return pl.pallas_call(
        _proj_sample_kernel,
        grid=(n // rows,),
        in_specs=[tile(D_MODEL), const(1, D_MODEL), const(D_MODEL, MAIN_WIDTH), const(D_MODEL, LANES),
                  const(1, LANES), const(1, A_WIDTH), const(1, A_WIDTH), const(A_GROUPS, CHUNK, CHUNK),
                  const(CHUNK, A_GROUPS)],
        out_specs=out_specs,
        out_shape=out_shape,
        compiler_params=pltpu.CompilerParams(dimension_semantics=("arbitrary",),
                                             vmem_limit_bytes=VMEM_LIMIT),
        name="proj_sample",
    )(x, g, w16, wf16, bfp, lng, lnb, ws, bsT)


def _past_decay_kernel(pt_ref, *refs):
    f_refs, o_ref = refs[:-1], refs[-1]
    n = len(f_refs)
    lf = jnp.stack([r[0] for r in f_refs], axis=0)
    r_i = lax.broadcasted_iota(jnp.int32, (PAGE_SIZE, 2 * PAGE_SIZE), 0)
    c_i = lax.broadcasted_iota(jnp.int32, (PAGE_SIZE, 2 * PAGE_SIZE), 1)
    tri = jnp.where((r_i >= c_i) | (c_i >= PAGE_SIZE), 1.0, 0.0).astype(bf16)
    sums = sum(jnp.dot(part.reshape(n * B_HEADS, PAGE_SIZE).astype(bf16), tri, preferred_element_type=f32)
               for part in _split3_bf16(lf))
    incl = sums[:, :PAGE_SIZE].reshape(lf.shape)
    later = sums[:, PAGE_SIZE:].reshape(lf.shape)
    later = jnp.concatenate([later[1:], jnp.zeros_like(later[:1])], axis=0)
    sh = 1
    while sh < n:
        later = later + jnp.concatenate([later[sh:], jnp.zeros_like(later[:sh])], axis=0)
        sh *= 2
    d = (incl - lf + later) * LOG2E
    for p in range(n):
        o_ref[0, :, p * PAGE_SIZE:(p + 1) * PAGE_SIZE] = d[p]


def _past_decay(page_table, lf_cache):
    nb, n_pages = page_table.shape

    def page_map(i):
        return lambda b, pt: (pt[b, i], 0, 0)

    grid_spec = pltpu.PrefetchScalarGridSpec(
        num_scalar_prefetch=1,
        grid=(nb,),
        in_specs=[pl.BlockSpec((1, B_HEADS, PAGE_SIZE), page_map(i)) for i in range(n_pages)],
        out_specs=pl.BlockSpec((1, B_HEADS, n_pages * PAGE_SIZE), lambda b, pt: (b, 0, 0)),
    )
    return pl.pallas_call(
        _past_decay_kernel,
        grid_spec=grid_spec,
        out_shape=jax.ShapeDtypeStruct((nb, B_HEADS, n_pages * PAGE_SIZE), f32),
        compiler_params=pltpu.CompilerParams(dimension_semantics=("arbitrary",)),
        name="past_decay",
    )(page_table, *([lf_cache] * n_pages))


def _decode_kernel(pt_ref, *refs):
    npg = PAGES_PER_STEP
    k_refs = refs[0:npg]
    v_refs = refs[npg:2 * npg]
    d_ref, q_ref, kaug_ref, v16_ref, o_ref, qbd_ref, acc_ref, m_ref, l_ref = refs[2 * npg:]
    j = pl.program_id(1)
    t = 8
    rows = B_HEADS * t
    width = npg * PAGE_SIZE

    @pl.when(j == 0)
    def _():
        qrep = jnp.concatenate([q_ref[0]] * B_HEADS, axis=0)
        r_i = lax.broadcasted_iota(jnp.int32, (rows, B_WIDTH), 0)
        c_i = lax.broadcasted_iota(jnp.int32, (rows, B_WIDTH), 1)
        qbd_ref[:, 0:B_WIDTH] = jnp.where((r_i // t) == (c_i // B_HEAD_DIM), qrep, 0.0).astype(bf16)
        r_b = lax.broadcasted_iota(jnp.int32, (rows, LANES), 0) // t
        c_b = lax.broadcasted_iota(jnp.int32, (rows, LANES), 1)
        ones = (c_b == r_b) | (c_b == r_b + B_HEADS) | (c_b == r_b + 2 * B_HEADS)
        qbd_ref[:, B_WIDTH:B_WIDTH + LANES] = jnp.where(ones, 1.0, 0.0).astype(bf16)
        m_ref[...] = jnp.full_like(m_ref, NEG_BIG)
        l_ref[...] = jnp.zeros_like(l_ref)
        acc_ref[...] = jnp.zeros_like(acc_ref)

    def update(s, vals, contract_lanes):
        m_old = m_ref[...]
        m_new = jnp.maximum(m_old, jnp.max(s, axis=1, keepdims=True))
        p = jnp.exp2(s - m_new)
        alpha = jnp.exp2(m_old - m_new)
        l_ref[...] = alpha * l_ref[...] + jnp.sum(p, axis=1, keepdims=True)
        if contract_lanes:
            pv = lax.dot_general(p.astype(bf16), vals, (((1,), (1,)), ((), ())), preferred_element_type=f32)
        else:
            pv = jnp.dot(p.astype(bf16), vals, preferred_element_type=f32)
        acc_ref[...] = alpha * acc_ref[...] + pv
        m_ref[...] = m_new

    kt = jnp.concatenate([r[0].astype(bf16) for r in k_refs], axis=1)
    vt = jnp.concatenate([r[0].astype(bf16) for r in v_refs], axis=1)
    d = d_ref[0]
    dfull = jnp.concatenate([jnp.broadcast_to(d[hh:hh + 1, :], (t, width)) for hh in range(B_HEADS)], axis=0)
    s = jnp.dot(qbd_ref[:, 0:B_WIDTH], kt, preferred_element_type=f32) + dfull
    update(s, vt, True)

    @pl.when(j == pl.num_programs(1) - 1)
    def _():
        sn = lax.dot_general(qbd_ref[...], kaug_ref[0], (((1,), (1,)), ((), ())),
                             preferred_element_type=f32)
        tq = lax.broadcasted_iota(jnp.int32, sn.shape, 0) % t
        tk = lax.broadcasted_iota(jnp.int32, sn.shape, 1)
        update(jnp.where(tk <= tq, sn, NEG_BIG), v16_ref[0], False)
        res = acc_ref[...] / l_ref[...]
        r_i = lax.broadcasted_iota(jnp.int32, (t, B_WIDTH), 1) // B_HEAD_DIM
        out = jnp.zeros((t, B_WIDTH), f32)
        for hh in range(B_HEADS):
            out = out + jnp.where(r_i == hh, res[hh * t:(hh + 1) * t, :], 0.0)
        o_ref[0] = out


def _decode_attention(page_table, kT_cache, vT_cache, decay, q, kaug, v16):
    nb, n_pages = page_table.shape
    npg = PAGES_PER_STEP
    steps = n_pages // npg
    t = 8

    def page_map(i):
        return lambda b, j, pt: (pt[b, j * npg + i], 0, 0)

    per_seq = lambda width: pl.BlockSpec((1, t, width), lambda b, j, pt: (b, 0, 0))
    in_specs = ([pl.BlockSpec((1, B_WIDTH, PAGE_SIZE), page_map(i)) for i in range(npg)]
                + [pl.BlockSpec((1, B_WIDTH, PAGE_SIZE), page_map(i)) for i in range(npg)]
                + [pl.BlockSpec((1, B_HEADS, npg * PAGE_SIZE), lambda b, j, pt: (b, 0, j))]
                + [per_seq(B_WIDTH), per_seq(B_WIDTH + LANES), per_seq(B_WIDTH)])
    grid_spec = pltpu.PrefetchScalarGridSpec(
        num_scalar_prefetch=1,
        grid=(nb, steps),
        in_specs=in_specs,
        out_specs=per_seq(B_WIDTH),
        scratch_shapes=[pltpu.VMEM((B_HEADS * t, B_WIDTH + LANES), bf16),
                        pltpu.VMEM((B_HEADS * t, B_WIDTH), f32),
                        pltpu.VMEM((B_HEADS * t, 1), f32), pltpu.VMEM((B_HEADS * t, 1), f32)],
    )
    return pl.pallas_call(
        _decode_kernel,
        grid_spec=grid_spec,
        out_shape=jax.ShapeDtypeStruct((nb, t, B_WIDTH), f32),
        compiler_params=pltpu.CompilerParams(dimension_semantics=("arbitrary", "arbitrary"),
                                             vmem_limit_bytes=VMEM_LIMIT),
        name="decode_attention",
    )(page_table, *([kT_cache] * npg), *([vT_cache] * npg), decay,
      q.reshape(nb, t, B_WIDTH), kaug.reshape(nb, t, B_WIDTH + LANES), v16.reshape(nb, t, B_WIDTH))


def _finish_kernel(transposed, x_ref, ga_ref, o_ref, gzb_ref, p_ref, wo_ref, gpost_ref, wpg_ref, bpg_ref,
                   wpe_ref, y_ref):
    batched = x_ref.ndim == 3
    ld = (lambda r: r[0]) if batched else (lambda r: r[...])
    x = ld(x_ref)
    o = ld(o_ref)
    if transposed:
        o = o.T
    mix_b = (o * ld(gzb_ref)).astype(bf16)
    y = (jnp.dot(ld(ga_ref), wo_ref[0:A_WIDTH, :], preferred_element_type=f32)
         + jnp.dot(mix_b, wo_ref[A_WIDTH:, :], preferred_element_type=f32))
    ms = jnp.mean(y * y, axis=-1, keepdims=True)
    x = x + y * lax.rsqrt(ms + RMS_EPS) * gpost_ref[...]
    gate = jax.nn.sigmoid(jnp.dot(x.astype(bf16), wpg_ref[...], preferred_element_type=f32) + bpg_ref[...])
    res = x + gate * jnp.dot(ld(p_ref).astype(bf16), wpe_ref[...], preferred_element_type=f32)
    if batched:
        y_ref[0] = res
    else:
        y_ref[...] = res


def _finish_prompt(x, ga, oT, gzb, p, wo16, gpost, wpg16, bpg, wpe16):
    bsz, seq, _ = x.shape
    rows = PROJ_ROWS
    const = lambda *shape: pl.BlockSpec(shape, lambda b, i: (0,) * len(shape))
    tile_rows = lambda width: pl.BlockSpec((1, rows, width), lambda b, i: (b, i, 0))
    return pl.pallas_call(
        functools.partial(_finish_kernel, True),
        grid=(bsz, seq // rows),
        in_specs=[tile_rows(D_MODEL), tile_rows(A_WIDTH),
                  pl.BlockSpec((1, B_WIDTH, rows), lambda b, i: (b, 0, i)),
                  tile_rows(B_WIDTH), tile_rows(PLE_DIM), const(D_MODEL, D_MODEL), const(1, D_MODEL),
                  const(D_MODEL, D_MODEL), const(1, D_MODEL), const(PLE_DIM, D_MODEL)],
        out_specs=tile_rows(D_MODEL),
        out_shape=jax.ShapeDtypeStruct(x.shape, f32),
        compiler_params=pltpu.CompilerParams(dimension_semantics=("arbitrary", "arbitrary"),
                                             vmem_limit_bytes=VMEM_LIMIT),
        name="finish_prompt",
    )(x, ga, oT, gzb, p, wo16, gpost, wpg16, bpg, wpe16)


def _finish_sample(x, ga, o, gzb, p, wo16, gpost, wpg16, bpg, wpe16):
    n = x.shape[0]
    rows = PROJ_ROWS
    const = lambda *shape: pl.BlockSpec(shape, lambda i: (0,) * len(shape))
    tile = lambda width: pl.BlockSpec((rows, width), lambda i: (i, 0))
    return pl.pallas_call(
        functools.partial(_finish_kernel, False),
        grid=(n // rows,),
        in_specs=[tile(D_MODEL), tile(A_WIDTH), tile(B_WIDTH), tile(B_WIDTH), tile(PLE_DIM),
                  const(D_MODEL, D_MODEL), const(1, D_MODEL), const(D_MODEL, D_MODEL), const(1, D_MODEL),
                  const(PLE_DIM, D_MODEL)],
        out_specs=tile(D_MODEL),
        out_shape=jax.ShapeDtypeStruct(x.shape, f32),
        compiler_params=pltpu.CompilerParams(dimension_semantics=("arbitrary",),
                                             vmem_limit_bytes=VMEM_LIMIT),
        name="finish_sample",
    )(x, ga, o, gzb, p, wo16, gpost, wpg16, bpg, wpe16)


def kernel(x_prompt, x_sample, cache_k, cache_v, cache_logf, page_table, p_prompt, p_sample, ln_pre_g, w_in, b_f,
           ln_v_g, ln_v_b, w_s, b_s, w_out, ln_post_g, w_pe, w_pg, b_pg):
    assert w_in.shape[0] == 1, "single-layer step"
    bsz, seq, _ = x_prompt.shape
    nb, t, _ = x_sample.shape
    n_pool = cache_k.shape[1]

    w16 = w_in[0, :, :MAIN_WIDTH].astype(bf16)
    wf16 = jnp.pad(w_in[0, :, MAIN_WIDTH:], ((0, 0), (0, LANES - B_HEADS))).astype(bf16)
    bfp = jnp.pad(b_f[0], (0, LANES - B_HEADS)).reshape(1, LANES)
    g_pre = ln_pre_g[0].reshape(1, D_MODEL)
    lng = ln_v_g[0].reshape(1, A_WIDTH)
    lnb = ln_v_b[0].reshape(1, A_WIDTH)
    bsT = b_s[0].T
    wo16 = w_out[0].astype(bf16)
    wpg16 = w_pg[0].astype(bf16)
    wpe16 = w_pe[0].astype(bf16)
    gpost = ln_post_g[0].reshape(1, D_MODEL)
    bpg = b_pg[0].reshape(1, D_MODEL)

    kT, vT, lfT, qT16, k16, kb16, vT16, ga, gzb = _proj_prompt(x_prompt, g_pre, w16, wf16, bfp, lng, lnb, w_s[0], bsT)
    oT = _flash_prompt(k16, kb16, qT16, vT16)
    y_prompt = _finish_prompt(x_prompt, ga, oT, gzb, p_prompt[0], wo16, gpost, wpg16, bpg, wpe16)
    new_k_prompt = kT.reshape(bsz, B_HEADS, B_HEAD_DIM, seq).transpose(0, 3, 1, 2)[None]
    new_v_prompt = vT.reshape(bsz, B_HEADS, B_HEAD_DIM, seq).transpose(0, 3, 1, 2)[None]
    new_logf_prompt = lfT.transpose(0, 2, 1)[None]

    kT_cache = cache_k[0].transpose(0, 2, 3, 1).reshape(n_pool, B_WIDTH, PAGE_SIZE)
    vT_cache = cache_v[0].transpose(0, 2, 3, 1).reshape(n_pool, B_WIDTH, PAGE_SIZE)
    lf_cache = cache_logf[0].transpose(0, 2, 1)
    xs = x_sample.reshape(nb * t, D_MODEL)
    ks, vs, lfs, vns, qs, kaug, vs16, gas, gzbs = _proj_sample(xs, g_pre, w16, wf16, bfp, lng, lnb, w_s[0], bsT)
    decay = _past_decay(page_table, lf_cache)
    o_s = _decode_attention(page_table, kT_cache, vT_cache, decay, qs, kaug, vs16)
    y_sample = _finish_sample(xs, gas, o_s.reshape(nb * t, B_WIDTH), gzbs, p_sample[0].reshape(nb * t, PLE_DIM),
                              wo16, gpost, wpg16, bpg, wpe16).reshape(nb, t, D_MODEL)

    return (y_prompt, y_sample, new_k_prompt, new_v_prompt, new_logf_prompt,
            ks.reshape(1, nb, t, B_HEADS, B_HEAD_DIM), vs.reshape(1, nb, t, B_HEADS, B_HEAD_DIM),
            lfs[:, :B_HEADS].reshape(1, nb, t, B_HEADS), vns.reshape(1, nb, t, A_GROUPS, CHUNK))
```

```python
import functools
import math

import jax
import jax.numpy as jnp
from jax import lax
from jax.experimental import pallas as pl
from jax.experimental.pallas import tpu as pltpu

D_MODEL = 1024
A_WIDTH = 512
B_WIDTH = 512
CHUNK = 128
A_GROUPS = 4
B_HEADS = 8
B_HEAD_DIM = 64
PLE_DIM = 256
PAGE_SIZE = 128
RMS_EPS = 1e-6
LN_EPS = 1e-5
MAIN_WIDTH = 3 * A_WIDTH + 4 * B_WIDTH
LANES = 128
LOG2E = math.log2(math.e)
QK_SCALE = B_HEAD_DIM ** -0.5
NEG_BIG = -1e30

PROJ_ROWS = 512
Q_TILE = 1024
K_SUB = 256
SUM_ROWS = 16
LOOKAHEAD = 3
PAGES_PER_STEP = 16
DECODE_GROUPS = 4
VMEM_LIMIT = 56 * 1024 * 1024

f32 = jnp.float32
bf16 = jnp.bfloat16


def _log_sigmoid(x):
    return jnp.minimum(x, 0.0) - jnp.log1p(jnp.exp(-jnp.abs(x)))


def _split3_bf16(x):
    hi = x.astype(bf16).astype(f32)
    r = x - hi
    mid = r.astype(bf16).astype(f32)
    lo = (r - mid).astype(bf16).astype(f32)
    return hi, mid, lo


def _bias_columns(nb):
    hi, mid, lo = _split3_bf16(nb)
    packed = hi + pltpu.roll(mid, B_HEADS, axis=1) + pltpu.roll(lo, 2 * B_HEADS, axis=1)
    return packed.astype(bf16)


def _project(x, g_ref, w_ref, wf_ref, bf_ref, lng_ref, lnb_ref):
    ms = jnp.mean(x * x, axis=-1, keepdims=True)
    h = (x * lax.rsqrt(ms + RMS_EPS) * g_ref[...]).astype(bf16)

    def sec(i):
        return jnp.dot(h, w_ref[:, i * 512:(i + 1) * 512], preferred_element_type=f32)

    ua = jax.nn.gelu(sec(0))
    va = jax.nn.gelu(sec(1))
    za = sec(2)
    q = sec(3)
    k = sec(4)
    v = sec(5)
    zb = sec(6)
    f = jnp.dot(h, wf_ref[...], preferred_element_type=f32) + bf_ref[...]
    col = lax.broadcasted_iota(jnp.int32, f.shape, 1)
    logf = jnp.where(col < B_HEADS, _log_sigmoid(f), 0.0)
    vn = []
    for g in range(A_GROUPS):
        vg = va[:, g * CHUNK:(g + 1) * CHUNK]
        mu = jnp.mean(vg, axis=-1, keepdims=True)
        var = jnp.mean(jnp.square(vg - mu), axis=-1, keepdims=True)
        vn.append((vg - mu) * lax.rsqrt(var + LN_EPS) * lng_ref[:, g * CHUNK:(g + 1) * CHUNK]
                  + lnb_ref[:, g * CHUNK:(g + 1) * CHUNK])
    return ua, vn, za, q, k, v, zb, logf


def _lane_cumsum(x):
    n = x.shape[-1]
    idx = lax.broadcasted_iota(jnp.int32, x.shape, x.ndim - 1)
    sh = 1
    while sh < n:
        x = x + jnp.where(idx >= sh, pltpu.roll(x, sh, axis=x.ndim - 1), 0.0)
        sh *= 2
    return x


def _lane_suffix_sum(x):
    n = x.shape[-1]
    idx = lax.broadcasted_iota(jnp.int32, x.shape, x.ndim - 1)
    sh = 1
    while sh < n:
        x = x + jnp.where(idx < n - sh, pltpu.roll(x, n - sh, axis=x.ndim - 1), 0.0)
        sh *= 2
    return x


def _proj_prompt_kernel(x_ref, g_ref, w_ref, wf_ref, bf_ref, lng_ref, lnb_ref, ws_ref, bsT_ref,
                        kT_ref, vT_ref, lfT_ref, qT16_ref, k16_ref, kb16_ref, vT16_ref, ga_ref, gzb_ref,
                        carry_ref):
    rows = x_ref.shape[1]

    @pl.when(pl.program_id(1) == 0)
    def _():
        carry_ref[...] = jnp.zeros_like(carry_ref)

    ua, vn, za, q, k, v, zb, logf = _project(x_ref[0], g_ref, w_ref, wf_ref, bf_ref, lng_ref, lnb_ref)

    tri = (lax.broadcasted_iota(jnp.int32, (CHUNK, CHUNK), 0)
           >= lax.broadcasted_iota(jnp.int32, (CHUNK, CHUNK), 1))
    mixed = []
    for g in range(A_GROUPS):
        wm = jnp.where(tri, ws_ref[g], 0.0).astype(bf16)
        bcol = jnp.broadcast_to(bsT_ref[:, g:g + 1], (CHUNK, CHUNK))
        vg = vn[g].astype(bf16)
        parts = [jnp.dot(wm, vg[c * CHUNK:(c + 1) * CHUNK], preferred_element_type=f32) + bcol
                 for c in range(rows // CHUNK)]
        mixed.append(jnp.concatenate(parts, axis=0))
    mix = jnp.concatenate(mixed, axis=1)
    ga_ref[0] = (ua * mix * jax.nn.silu(za)).astype(bf16)
    gzb_ref[0] = jax.nn.silu(zb)

    kT = k.T
    vT = v.T
    kT_ref[0] = kT
    vT_ref[0] = vT
    vT16_ref[0] = vT.astype(bf16)
    k16_ref[0] = k.astype(bf16)
    qT16_ref[0] = (q * (QK_SCALE * LOG2E)).T.astype(bf16)

    lfT = logf.T[0:B_HEADS, :]
    lfT_ref[0] = lfT
    cT = _lane_cumsum(lfT) + carry_ref[:, 0:1]
    carry_ref[...] = jnp.broadcast_to(cT[:, rows - 1:rows], carry_ref.shape)
    c_rows = jnp.concatenate([cT, jnp.zeros((LANES - B_HEADS, rows), f32)], axis=0).T
    kb16_ref[0] = _bias_columns(c_rows * (-LOG2E))


def _proj_prompt(x, g, w16, wf16, bfp, lng, lnb, ws, bsT):
    bsz, seq, _ = x.shape
    rows = PROJ_ROWS
    const = lambda *shape: pl.BlockSpec(shape, lambda b, i: (0,) * len(shape))
    tile_rows = lambda width: pl.BlockSpec((1, rows, width), lambda b, i: (b, i, 0))
    tile_cols = lambda height: pl.BlockSpec((1, height, rows), lambda b, i: (b, 0, i))
    out_shape = (
        jax.ShapeDtypeStruct((bsz, B_WIDTH, seq), f32),
        jax.ShapeDtypeStruct((bsz, B_WIDTH, seq), f32),
        jax.ShapeDtypeStruct((bsz, B_HEADS, seq), f32),
        jax.ShapeDtypeStruct((bsz, B_WIDTH, seq), bf16),
        jax.ShapeDtypeStruct((bsz, seq, B_WIDTH), bf16),
        jax.ShapeDtypeStruct((bsz, seq, LANES), bf16),
        jax.ShapeDtypeStruct((bsz, B_WIDTH, seq), bf16),
        jax.ShapeDtypeStruct((bsz, seq, A_WIDTH), bf16),
        jax.ShapeDtypeStruct((bsz, seq, B_WIDTH), f32),
    )
    out_specs = (tile_cols(B_WIDTH), tile_cols(B_WIDTH), tile_cols(B_HEADS), tile_cols(B_WIDTH),
                 tile_rows(B_WIDTH), tile_rows(LANES), tile_cols(B_WIDTH), tile_rows(A_WIDTH), tile_rows(B_WIDTH))
    return pl.pallas_call(
        _proj_prompt_kernel,
        grid=(bsz, seq // rows),
        in_specs=[tile_rows(D_MODEL), const(1, D_MODEL), const(D_MODEL, MAIN_WIDTH), const(D_MODEL, LANES),
                  const(1, LANES), const(1, A_WIDTH), const(1, A_WIDTH), const(A_GROUPS, CHUNK, CHUNK),
                  const(CHUNK, A_GROUPS)],
        out_specs=out_specs,
        out_shape=out_shape,
        scratch_shapes=[pltpu.VMEM((B_HEADS, LANES), f32)],
        compiler_params=pltpu.CompilerParams(dimension_semantics=("arbitrary", "arbitrary"),
                                             vmem_limit_bytes=VMEM_LIMIT),
        name="proj_prompt",
    )(x, g, w16, wf16, bfp, lng, lnb, ws, bsT)


def _flash_kernel(kp_ref, kb_ref, qT_ref, vT_ref, o_ref, qaug_ref, acc_ref, m_ref):
    h = pl.program_id(1)
    par = h % 2
    seq = kp_ref.shape[1]
    t = Q_TILE
    ks = K_SUB
    row = lax.broadcasted_iota(jnp.int32, (LANES, t), 0)
    ones_rows = jnp.where((row == h) | (row == h + B_HEADS) | (row == h + 2 * B_HEADS), 1.0, 0.0).astype(bf16)
    causal = lax.broadcasted_iota(jnp.int32, (ks, t), 0) <= lax.broadcasted_iota(jnp.int32, (ks, t), 1)
    sum_rows = jnp.ones((SUM_ROWS, ks), bf16)

    def scores(k0, lo):
        kt = jnp.concatenate([kp_ref[0, pl.ds(k0, ks), :], kb_ref[0, pl.ds(k0, ks), :]], axis=1)
        return jnp.dot(kt, qaug_ref[:, lo:], preferred_element_type=f32)

    def consume(s, k0, lo, masked):
        if masked:
            s = jnp.where(causal[:, :t - lo], s, NEG_BIG)
        m_old = m_ref[:, lo:]
        m_new = jnp.maximum(m_old, jnp.max(s, axis=0, keepdims=True))
        p = jnp.exp2(s - m_new).astype(bf16)
        alpha = jnp.exp2(m_old - m_new)
        v_aug = jnp.concatenate([vT_ref[0, :, pl.ds(k0, ks)], sum_rows], axis=0)
        acc_ref[:, lo:] = alpha * acc_ref[:, lo:] + jnp.dot(v_aug, p, preferred_element_type=f32)
        m_ref[:, lo:] = m_new

    def q_body(qi, carry):
        q0 = pl.multiple_of(qi * t, t)
        qp = qT_ref[0, :, pl.ds(q0, t)].astype(f32)
        qaug_ref[0:LANES, :] = jnp.where((row // B_HEAD_DIM) == par, qp, 0.0).astype(bf16)
        qaug_ref[LANES:2 * LANES, :] = ones_rows
        m_ref[...] = jnp.full_like(m_ref, NEG_BIG)
        acc_ref[...] = jnp.zeros_like(acc_ref)

        nsub = t // ks

        def kv_body(kj, c):
            base = kj * t
            ahead = [scores(pl.multiple_of(base + r * ks, ks), 0) for r in range(min(LOOKAHEAD, nsub))]
            for r in range(nsub):
                if r + LOOKAHEAD < nsub:
                    ahead.append(scores(pl.multiple_of(base + (r + LOOKAHEAD) * ks, ks), 0))
                consume(ahead.pop(0), pl.multiple_of(base + r * ks, ks), 0, False)
            return c

        lax.fori_loop(0, qi, kv_body, 0)
        ahead = [scores(pl.multiple_of(q0 + r * ks, ks), r * ks) for r in range(min(LOOKAHEAD, nsub))]
        for r in range(nsub):
            if r + LOOKAHEAD < nsub:
                ahead.append(scores(pl.multiple_of(q0 + (r + LOOKAHEAD) * ks, ks), (r + LOOKAHEAD) * ks))
            consume(ahead.pop(0), pl.multiple_of(q0 + r * ks, ks), r * ks, True)
        o_ref[0, :, pl.ds(q0, t)] = acc_ref[0:B_HEAD_DIM, :] / acc_ref[B_HEAD_DIM:B_HEAD_DIM + 1, :]
        return carry

    lax.fori_loop(0, seq // t, q_body, 0)


def _flash_prompt(k16, kb16, qT16, vT16):
    bsz, seq, _ = k16.shape
    t = Q_TILE
    return pl.pallas_call(
        _flash_kernel,
        grid=(bsz, B_HEADS),
        in_specs=[pl.BlockSpec((1, seq, LANES), lambda b, h: (b, 0, h // 2)),
                  pl.BlockSpec((1, seq, LANES), lambda b, h: (b, 0, 0)),
                  pl.BlockSpec((1, LANES, seq), lambda b, h: (b, h // 2, 0)),
                  pl.BlockSpec((1, B_HEAD_DIM, seq), lambda b, h: (b, h, 0))],
        out_specs=pl.BlockSpec((1, B_HEAD_DIM, seq), lambda b, h: (b, h, 0)),
        out_shape=jax.ShapeDtypeStruct((bsz, B_WIDTH, seq), f32),
        scratch_shapes=[pltpu.VMEM((2 * LANES, t), bf16), pltpu.VMEM((B_HEAD_DIM + SUM_ROWS, t), f32),
                        pltpu.VMEM((1, t), f32)],
        compiler_params=pltpu.CompilerParams(dimension_semantics=("arbitrary", "arbitrary"),
                                             vmem_limit_bytes=VMEM_LIMIT),
        name="flash_prompt",
    )(k16, kb16, qT16, vT16)


def _proj_sample_kernel(x_ref, g_ref, w_ref, wf_ref, bf_ref, lng_ref, lnb_ref, ws_ref, bsT_ref,
                        k_ref, v_ref, lf_ref, vn_ref, q_ref, kaug_ref, v16_ref, ga_ref, gzb_ref):
    rows = x_ref.shape[0]
    t = 8
    ua, vn, za, q, k, v, zb, logf = _project(x_ref[...], g_ref, w_ref, wf_ref, bf_ref, lng_ref, lnb_ref)

    r_i = lax.broadcasted_iota(jnp.int32, (rows, rows), 0)
    c_i = lax.broadcasted_iota(jnp.int32, (rows, rows), 1)
    block = ((r_i // t) == (c_i // t)) & ((c_i % t) <= (r_i % t))
    spread = (lax.broadcasted_iota(jnp.int32, (LANES, rows), 0)
              == lax.broadcasted_iota(jnp.int32, (LANES, rows), 1) % t)
    spread = jnp.where(spread, 1.0, 0.0).astype(bf16)
    mixed = []
    for g in range(A_GROUPS):
        w8 = jnp.dot(ws_ref[g, 0:t, :].astype(bf16), spread, preferred_element_type=f32)
        wt = jnp.broadcast_to(w8[None], (rows // t, t, rows)).reshape(rows, rows)
        wbd = jnp.where(block, wt, 0.0).astype(bf16)
        b8 = jnp.broadcast_to(bsT_ref[0:t, g:g + 1], (t, CHUNK))
        bcol = jnp.broadcast_to(b8[None], (rows // t, t, CHUNK)).reshape(rows, CHUNK)
        mixed.append(jnp.dot(wbd, vn[g].astype(bf16), preferred_element_type=f32) + bcol)
    mix = jnp.concatenate(mixed, axis=1)
    ga_ref[...] = (ua * mix * jax.nn.silu(za)).astype(bf16)
    gzb_ref[...] = jax.nn.silu(zb)
    vn_ref[...] = jnp.concatenate(vn, axis=1)
    k_ref[...] = k
    v_ref[...] = v
    v16_ref[...] = v.astype(bf16)
    lf_ref[...] = logf
    q_ref[...] = q * (QK_SCALE * LOG2E)

    lf3 = logf.reshape(rows // t, t, LANES)
    tok = lax.broadcasted_iota(jnp.int32, lf3.shape, 1)
    c3 = jnp.zeros_like(lf3)
    for s in range(t):
        c3 = c3 + jnp.where(tok >= s, jnp.broadcast_to(lf3[:, s:s + 1, :], lf3.shape), 0.0)
    c_rows = c3.reshape(rows, LANES)
    kaug_ref[:, 0:B_WIDTH] = k.astype(bf16)
    kaug_ref[:, B_WIDTH:B_WIDTH + LANES] = _bias_columns(c_rows * (-LOG2E))


def _proj_sample(x, g, w16, wf16, bfp, lng, lnb, ws, bsT):
    n = x.shape[0]
    rows = PROJ_ROWS
    const = lambda *shape: pl.BlockSpec(shape, lambda i: (0,) * len(shape))
    tile = lambda width: pl.BlockSpec((rows, width), lambda i: (i, 0))
    out_shape = (
        jax.ShapeDtypeStruct((n, B_WIDTH), f32),
        jax.ShapeDtypeStruct((n, B_WIDTH), f32),
        jax.ShapeDtypeStruct((n, LANES), f32),
        jax.ShapeDtypeStruct((n, A_WIDTH), f32),
        jax.ShapeDtypeStruct((n, B_WIDTH), f32),
        jax.ShapeDtypeStruct((n, B_WIDTH + LANES), bf16),
        jax.ShapeDtypeStruct((n, B_WIDTH), bf16),
        jax.ShapeDtypeStruct((n, A_WIDTH), bf16),
        jax.ShapeDtypeStruct((n, B_WIDTH), f32),
    )
    out_specs = (tile(B_WIDTH), tile(B_WIDTH), tile(LANES), tile(A_WIDTH), tile(B_WIDTH),
                 tile(B_WIDTH + LANES), tile(B_WIDTH), tile(A_WIDTH), tile(B_WIDTH))
    return pl.pallas_call(
        _proj_sample_kernel,
        grid=(n // rows,),
        in_specs=[tile(D_MODEL), const(1, D_MODEL), const(D_MODEL, MAIN_WIDTH), const(D_MODEL, LANES),
                  const(1, LANES), const(1, A_WIDTH), const(1, A_WIDTH), const(A_GROUPS, CHUNK, CHUNK),
                  const(CHUNK, A_GROUPS)],
        out_specs=out_specs,
        out_shape=out_shape,
        compiler_params=pltpu.CompilerParams(dimension_semantics=("arbitrary",),
                                             vmem_limit_bytes=VMEM_LIMIT),
        name="proj_sample",
    )(x, g, w16, wf16, bfp, lng, lnb, ws, bsT)


def _past_decay_kernel(pt_ref, *refs):
    f_ref, o_ref = refs
    b = pl.program_id(0)
    n = o_ref.shape[2] // PAGE_SIZE
    lf = jnp.stack([f_ref[pt_ref[b, p]] for p in range(n)], axis=0)
    r_i = lax.broadcasted_iota(jnp.int32, (PAGE_SIZE, 2 * PAGE_SIZE), 0)
    c_i = lax.broadcasted_iota(jnp.int32, (PAGE_SIZE, 2 * PAGE_SIZE), 1)
    tri = jnp.where((r_i >= c_i) | (c_i >= PAGE_SIZE), 1.0, 0.0).astype(bf16)
    sums = sum(jnp.dot(part.reshape(n * B_HEADS, PAGE_SIZE).astype(bf16), tri, preferred_element_type=f32)
               for part in _split3_bf16(lf))
    incl = sums[:, :PAGE_SIZE].reshape(lf.shape)
    later = sums[:, PAGE_SIZE:].reshape(lf.shape)
    later = jnp.concatenate([later[1:], jnp.zeros_like(later[:1])], axis=0)
    sh = 1
    while sh < n:
        later = later + jnp.concatenate([later[sh:], jnp.zeros_like(later[:sh])], axis=0)
        sh *= 2
    d = (incl - lf + later) * LOG2E
    for p in range(n):
        o_ref[0, :, p * PAGE_SIZE:(p + 1) * PAGE_SIZE] = d[p]


def _past_decay(page_table, lf_cache):
    nb, n_pages = page_table.shape

    grid_spec = pltpu.PrefetchScalarGridSpec(
        num_scalar_prefetch=1,
        grid=(nb,),
        in_specs=[pl.BlockSpec(memory_space=pltpu.VMEM)],
        out_specs=pl.BlockSpec((1, B_HEADS, n_pages * PAGE_SIZE), lambda b, pt: (b, 0, 0)),
    )
    return pl.pallas_call(
        _past_decay_kernel,
        grid_spec=grid_spec,
        out_shape=jax.ShapeDtypeStruct((nb, B_HEADS, n_pages * PAGE_SIZE), f32),
        compiler_params=pltpu.CompilerParams(dimension_semantics=("arbitrary",), vmem_limit_bytes=VMEM_LIMIT),
        name="past_decay",
    )(page_table, lf_cache)


def _decode_kernel(pt_ref, *refs):
    npg = PAGES_PER_STEP
    k_refs = refs[0:npg]
    v_refs = refs[npg:2 * npg]
    d_ref, q_ref, kaug_ref, v16_ref, o_ref, qbd_ref, acc_ref, m_ref, l_ref = refs[2 * npg:]
    j = pl.program_id(1)
    t = 8
    rows = B_HEADS * t
    width = npg * PAGE_SIZE

    @pl.when(j == 0)
    def _():
        qrep = jnp.concatenate([q_ref[0]] * B_HEADS, axis=0)
        r_i = lax.broadcasted_iota(jnp.int32, (rows, B_WIDTH), 0)
        c_i = lax.broadcasted_iota(jnp.int32, (rows, B_WIDTH), 1)
        qbd_ref[:, 0:B_WIDTH] = jnp.where((r_i // t) == (c_i // B_HEAD_DIM), qrep, 0.0).astype(bf16)
        r_b = lax.broadcasted_iota(jnp.int32, (rows, LANES), 0) // t
        c_b = lax.broadcasted_iota(jnp.int32, (rows, LANES), 1)
        ones = (c_b == r_b) | (c_b == r_b + B_HEADS) | (c_b == r_b + 2 * B_HEADS)
        qbd_ref[:, B_WIDTH:B_WIDTH + LANES] = jnp.where(ones, 1.0, 0.0).astype(bf16)
        m_ref[...] = jnp.full_like(m_ref, NEG_BIG)
        l_ref[...] = jnp.zeros_like(l_ref)
        acc_ref[...] = jnp.zeros_like(acc_ref)

    def update(s, vals, contract_lanes):
        m_old = m_ref[...]
        m_new = jnp.maximum(m_old, jnp.max(s, axis=1, keepdims=True))
        p = jnp.exp2(s - m_new)
        alpha = jnp.exp2(m_old - m_new)
        l_ref[...] = alpha * l_ref[...] + jnp.sum(p, axis=1, keepdims=True)
        if contract_lanes:
            pv = lax.dot_general(p.astype(bf16), vals, (((1,), (1,)), ((), ())), preferred_element_type=f32)
        else:
            pv = jnp.dot(p.astype(bf16), vals, preferred_element_type=f32)
        acc_ref[...] = alpha * acc_ref[...] + pv
        m_ref[...] = m_new

    d = d_ref[0]
    dfull = jnp.concatenate([jnp.broadcast_to(d[hh:hh + 1, :], (t, width)) for hh in range(B_HEADS)], axis=0)
    gp = npg // DECODE_GROUPS
    gw = gp * PAGE_SIZE
    pages16 = lambda page_refs, g: jnp.concatenate([r[0].astype(bf16) for r in page_refs[g * gp:(g + 1) * gp]], axis=1)
    scores = [jnp.dot(qbd_ref[:, 0:B_WIDTH], pages16(k_refs, g), preferred_element_type=f32)
              + dfull[:, g * gw:(g + 1) * gw] for g in range(DECODE_GROUPS)]
    for g in range(DECODE_GROUPS):
        update(scores[g], pages16(v_refs, g), True)

    @pl.when(j == pl.num_programs(1) - 1)
    def _():
        sn = lax.dot_general(qbd_ref[...], kaug_ref[0], (((1,), (1,)), ((), ())),
                             preferred_element_type=f32)
        tq = lax.broadcasted_iota(jnp.int32, sn.shape, 0) % t
        tk = lax.broadcasted_iota(jnp.int32, sn.shape, 1)
        update(jnp.where(tk <= tq, sn, NEG_BIG), v16_ref[0], False)
        res = acc_ref[...] / l_ref[...]
        r_i = lax.broadcasted_iota(jnp.int32, (t, B_WIDTH), 1) // B_HEAD_DIM
        out = jnp.zeros((t, B_WIDTH), f32)
        for hh in range(B_HEADS):
            out = out + jnp.where(r_i == hh, res[hh * t:(hh + 1) * t, :], 0.0)
        o_ref[0] = out


def _decode_attention(page_table, kT_cache, vT_cache, decay, q, kaug, v16):
    nb, n_pages = page_table.shape
    npg = PAGES_PER_STEP
    steps = n_pages // npg
    t = 8

    def page_map(i):
        return lambda b, j, pt: (pt[b, j * npg + i], 0, 0)

    per_seq = lambda width: pl.BlockSpec((1, t, width), lambda b, j, pt: (b, 0, 0))
    page_spec = lambda i: pl.BlockSpec((1, B_WIDTH, PAGE_SIZE), page_map(i))
    in_specs = ([page_spec(i) for i in range(npg)] + [page_spec(i) for i in range(npg)]
                + [pl.BlockSpec((1, B_HEADS, npg * PAGE_SIZE), lambda b, j, pt: (b, 0, j))]
                + [per_seq(B_WIDTH), per_seq(B_WIDTH + LANES), per_seq(B_WIDTH)])
    grid_spec = pltpu.PrefetchScalarGridSpec(
        num_scalar_prefetch=1,
        grid=(nb, steps),
        in_specs=in_specs,
        out_specs=per_seq(B_WIDTH),
        scratch_shapes=[pltpu.VMEM((B_HEADS * t, B_WIDTH + LANES), bf16),
                        pltpu.VMEM((B_HEADS * t, B_WIDTH), f32),
                        pltpu.VMEM((B_HEADS * t, 1), f32), pltpu.VMEM((B_HEADS * t, 1), f32)],
    )
    return pl.pallas_call(
        _decode_kernel,
        grid_spec=grid_spec,
        out_shape=jax.ShapeDtypeStruct((nb, t, B_WIDTH), f32),
        compiler_params=pltpu.CompilerParams(dimension_semantics=("arbitrary", "arbitrary"),
                                             vmem_limit_bytes=VMEM_LIMIT),
        name="decode_attention",
    )(page_table, *([kT_cache] * npg), *([vT_cache] * npg), decay,
      q.reshape(nb, t, B_WIDTH), kaug.reshape(nb, t, B_WIDTH + LANES), v16.reshape(nb, t, B_WIDTH))


def _finish_kernel(transposed, x_ref, ga_ref, o_ref, gzb_ref, p_ref, wo_ref, gpost_ref, wpg_ref, bpg_ref,
                   wpe_ref, y_ref):
    batched = x_ref.ndim == 3
    ld = (lambda r: r[0]) if batched else (lambda r: r[...])
    x = ld(x_ref)
    o = ld(o_ref)
    if transposed:
        o = o.T
    mix_b = (o * ld(gzb_ref)).astype(bf16)
    y = (jnp.dot(ld(ga_ref), wo_ref[0:A_WIDTH, :], preferred_element_type=f32)
         + jnp.dot(mix_b, wo_ref[A_WIDTH:, :], preferred_element_type=f32))
    ms = jnp.mean(y * y, axis=-1, keepdims=True)
    x = x + y * lax.rsqrt(ms + RMS_EPS) * gpost_ref[...]
    gate = jax.nn.sigmoid(jnp.dot(x.astype(bf16), wpg_ref[...], preferred_element_type=f32) + bpg_ref[...])
    res = x + gate * jnp.dot(ld(p_ref).astype(bf16), wpe_ref[...], preferred_element_type=f32)
    if batched:
        y_ref[0] = res
    else:
        y_ref[...] = res


def _finish_prompt(x, ga, oT, gzb, p, wo16, gpost, wpg16, bpg, wpe16):
    bsz, seq, _ = x.shape
    rows = PROJ_ROWS
    const = lambda *shape: pl.BlockSpec(shape, lambda b, i: (0,) * len(shape))
    tile_rows = lambda width: pl.BlockSpec((1, rows, width), lambda b, i: (b, i, 0))
    return pl.pallas_call(
        functools.partial(_finish_kernel, True),
        grid=(bsz, seq // rows),
        in_specs=[tile_rows(D_MODEL), tile_rows(A_WIDTH),
                  pl.BlockSpec((1, B_WIDTH, rows), lambda b, i: (b, 0, i)),
                  tile_rows(B_WIDTH), tile_rows(PLE_DIM), const(D_MODEL, D_MODEL), const(1, D_MODEL),
                  const(D_MODEL, D_MODEL), const(1, D_MODEL), const(PLE_DIM, D_MODEL)],
        out_specs=tile_rows(D_MODEL),
        out_shape=jax.ShapeDtypeStruct(x.shape, f32),
        compiler_params=pltpu.CompilerParams(dimension_semantics=("arbitrary", "arbitrary"),
                                             vmem_limit_bytes=VMEM_LIMIT),
        name="finish_prompt",
    )(x, ga, oT, gzb, p, wo16, gpost, wpg16, bpg, wpe16)


def _finish_sample(x, ga, o, gzb, p, wo16, gpost, wpg16, bpg, wpe16):
    n = x.shape[0]
    rows = PROJ_ROWS
    const = lambda *shape: pl.BlockSpec(shape, lambda i: (0,) * len(shape))
    tile = lambda width: pl.BlockSpec((rows, width), lambda i: (i, 0))
    return pl.pallas_call(
        functools.partial(_finish_kernel, False),
        grid=(n // rows,),
        in_specs=[tile(D_MODEL), tile(A_WIDTH), tile(B_WIDTH), tile(B_WIDTH), tile(PLE_DIM),
                  const(D_MODEL, D_MODEL), const(1, D_MODEL), const(D_MODEL, D_MODEL), const(1, D_MODEL),
                  const(PLE_DIM, D_MODEL)],
        out_specs=tile(D_MODEL),
        out_shape=jax.ShapeDtypeStruct(x.shape, f32),
        compiler_params=pltpu.CompilerParams(dimension_semantics=("arbitrary",),
                                             vmem_limit_bytes=VMEM_LIMIT),
        name="finish_sample",
    )(x, ga, o, gzb, p, wo16, gpost, wpg16, bpg, wpe16)


def kernel(x_prompt, x_sample, cache_k, cache_v, cache_logf, page_table, p_prompt, p_sample, ln_pre_g, w_in, b_f,
           ln_v_g, ln_v_b, w_s, b_s, w_out, ln_post_g, w_pe, w_pg, b_pg):
    assert w_in.shape[0] == 1, "single-layer step"
    bsz, seq, _ = x_prompt.shape
    nb, t, _ = x_sample.shape
    n_pool = cache_k.shape[1]

    w16 = w_in[0, :, :MAIN_WIDTH].astype(bf16)
    wf16 = jnp.pad(w_in[0, :, MAIN_WIDTH:], ((0, 0), (0, LANES - B_HEADS))).astype(bf16)
    bfp = jnp.pad(b_f[0], (0, LANES - B_HEADS)).reshape(1, LANES)
    g_pre = ln_pre_g[0].reshape(1, D_MODEL)
    lng = ln_v_g[0].reshape(1, A_WIDTH)
    lnb = ln_v_b[0].reshape(1, A_WIDTH)
    bsT = b_s[0].T
    wo16 = w_out[0].astype(bf16)
    wpg16 = w_pg[0].astype(bf16)
    wpe16 = w_pe[0].astype(bf16)
    gpost = ln_post_g[0].reshape(1, D_MODEL)
    bpg = b_pg[0].reshape(1, D_MODEL)

    kT, vT, lfT, qT16, k16, kb16, vT16, ga, gzb = _proj_prompt(x_prompt, g_pre, w16, wf16, bfp, lng, lnb, w_s[0], bsT)
    oT = _flash_prompt(k16, kb16, qT16, vT16)
    y_prompt = _finish_prompt(x_prompt, ga, oT, gzb, p_prompt[0], wo16, gpost, wpg16, bpg, wpe16)
    new_k_prompt = kT.reshape(bsz, B_HEADS, B_HEAD_DIM, seq).transpose(0, 3, 1, 2)[None]
    new_v_prompt = vT.reshape(bsz, B_HEADS, B_HEAD_DIM, seq).transpose(0, 3, 1, 2)[None]
    new_logf_prompt = lfT.transpose(0, 2, 1)[None]

    kT_cache = cache_k[0].transpose(0, 2, 3, 1).reshape(n_pool, B_WIDTH, PAGE_SIZE)
    vT_cache = cache_v[0].transpose(0, 2, 3, 1).reshape(n_pool, B_WIDTH, PAGE_SIZE)
    lf_cache = cache_logf[0].transpose(0, 2, 1)
    xs = x_sample.reshape(nb * t, D_MODEL)
    ks, vs, lfs, vns, qs, kaug, vs16, gas, gzbs = _proj_sample(xs, g_pre, w16, wf16, bfp, lng, lnb, w_s[0], bsT)
    decay = _past_decay(page_table, lf_cache)
    o_s = _decode_attention(page_table, kT_cache, vT_cache, decay, qs, kaug, vs16)
    y_sample = _finish_sample(xs, gas, o_s.reshape(nb * t, B_WIDTH), gzbs, p_sample[0].reshape(nb * t, PLE_DIM),
                              wo16, gpost, wpg16, bpg, wpe16).reshape(nb, t, D_MODEL)

    return (y_prompt, y_sample, new_k_prompt, new_v_prompt, new_logf_prompt,
            ks.reshape(1, nb, t, B_HEADS, B_HEAD_DIM), vs.reshape(1, nb, t, B_HEADS, B_HEAD_DIM),
            lfs[:, :B_HEADS].reshape(1, nb, t, B_HEADS), vns.reshape(1, nb, t, A_GROUPS, CHUNK))
```

```python
import functools
import math

import jax
import jax.numpy as jnp
from jax import lax
from jax.experimental import pallas as pl
from jax.experimental.pallas import tpu as pltpu

D_MODEL = 1024
A_WIDTH = 512
B_WIDTH = 512
CHUNK = 128
A_GROUPS = 4
B_HEADS = 8
B_HEAD_DIM = 64
PLE_DIM = 256
PAGE_SIZE = 128
RMS_EPS = 1e-6
LN_EPS = 1e-5
MAIN_WIDTH = 3 * A_WIDTH + 4 * B_WIDTH
LANES = 128
LOG2E = math.log2(math.e)
QK_SCALE = B_HEAD_DIM ** -0.5
NEG_BIG = -1e30

PROJ_ROWS = 512
Q_TILE = 1024
K_SUB = 256
SUM_ROWS = 16
LOOKAHEAD = 3
PAGES_PER_STEP = 16
DECODE_GROUPS = 4
VMEM_LIMIT = 56 * 1024 * 1024

f32 = jnp.float32
bf16 = jnp.bfloat16


def _log_sigmoid(x):
    return jnp.minimum(x, 0.0) - jnp.log1p(jnp.exp(-jnp.abs(x)))


def _split3_bf16(x):
    hi = x.astype(bf16).astype(f32)
    r = x - hi
    mid = r.astype(bf16).astype(f32)
    lo = (r - mid).astype(bf16).astype(f32)
    return hi, mid, lo


def _bias_columns(nb):
    hi, mid, lo = _split3_bf16(nb)
    packed = hi + pltpu.roll(mid, B_HEADS, axis=1) + pltpu.roll(lo, 2 * B_HEADS, axis=1)
    return packed.astype(bf16)


def _project(x, g_ref, w_ref, wf_ref, bf_ref, lng_ref, lnb_ref):
    ms = jnp.mean(x * x, axis=-1, keepdims=True)
    h = (x * lax.rsqrt(ms + RMS_EPS) * g_ref[...]).astype(bf16)

    def sec(i):
        return jnp.dot(h, w_ref[:, i * 512:(i + 1) * 512], preferred_element_type=f32)

    ua = jax.nn.gelu(sec(0))
    va = jax.nn.gelu(sec(1))
    za = sec(2)
    q = sec(3)
    k = sec(4)
    v = sec(5)
    zb = sec(6)
    f = jnp.dot(h, wf_ref[...], preferred_element_type=f32) + bf_ref[...]
    col = lax.broadcasted_iota(jnp.int32, f.shape, 1)
    logf = jnp.where(col < B_HEADS, _log_sigmoid(f), 0.0)
    vn = []
    for g in range(A_GROUPS):
        vg = va[:, g * CHUNK:(g + 1) * CHUNK]
        mu = jnp.mean(vg, axis=-1, keepdims=True)
        var = jnp.mean(jnp.square(vg - mu), axis=-1, keepdims=True)
        vn.append((vg - mu) * lax.rsqrt(var + LN_EPS) * lng_ref[:, g * CHUNK:(g + 1) * CHUNK]
                  + lnb_ref[:, g * CHUNK:(g + 1) * CHUNK])
    return ua, vn, za, q, k, v, zb, logf


def _lane_cumsum(x):
    n = x.shape[-1]
    idx = lax.broadcasted_iota(jnp.int32, x.shape, x.ndim - 1)
    sh = 1
    while sh < n:
        x = x + jnp.where(idx >= sh, pltpu.roll(x, sh, axis=x.ndim - 1), 0.0)
        sh *= 2
    return x


def _lane_suffix_sum(x):
    n = x.shape[-1]
    idx = lax.broadcasted_iota(jnp.int32, x.shape, x.ndim - 1)
    sh = 1
    while sh < n:
        x = x + jnp.where(idx < n - sh, pltpu.roll(x, n - sh, axis=x.ndim - 1), 0.0)
        sh *= 2
    return x


def _proj_prompt_kernel(x_ref, g_ref, w_ref, wf_ref, bf_ref, lng_ref, lnb_ref, ws_ref, bsT_ref,
                        kT_ref, vT_ref, lfT_ref, qT16_ref, k16_ref, kb16_ref, vT16_ref, ga_ref, gzb_ref,
                        carry_ref):
    rows = x_ref.shape[1]

    @pl.when(pl.program_id(1) == 0)
    def _():
        carry_ref[...] = jnp.zeros_like(carry_ref)

    ua, vn, za, q, k, v, zb, logf = _project(x_ref[0], g_ref, w_ref, wf_ref, bf_ref, lng_ref, lnb_ref)

    tri = (lax.broadcasted_iota(jnp.int32, (CHUNK, CHUNK), 0)
           >= lax.broadcasted_iota(jnp.int32, (CHUNK, CHUNK), 1))
    mixed = []
    for g in range(A_GROUPS):
        wm = jnp.where(tri, ws_ref[g], 0.0).astype(bf16)
        bcol = jnp.broadcast_to(bsT_ref[:, g:g + 1], (CHUNK, CHUNK))
        vg = vn[g].astype(bf16)
        parts = [jnp.dot(wm, vg[c * CHUNK:(c + 1) * CHUNK], preferred_element_type=f32) + bcol
                 for c in range(rows // CHUNK)]
        mixed.append(jnp.concatenate(parts, axis=0))
    mix = jnp.concatenate(mixed, axis=1)
    ga_ref[0] = (ua * mix * jax.nn.silu(za)).astype(bf16)
    gzb_ref[0] = jax.nn.silu(zb)

    kT = k.T
    vT = v.T
    kT_ref[0] = kT
    vT_ref[0] = vT
    vT16_ref[0] = vT.astype(bf16)
    k16_ref[0] = k.astype(bf16)
    qT16_ref[0] = (q * (QK_SCALE * LOG2E)).T.astype(bf16)

    lfT = logf.T[0:B_HEADS, :]
    lfT_ref[0] = lfT
    cT = _lane_cumsum(lfT) + carry_ref[:, 0:1]
    carry_ref[...] = jnp.broadcast_to(cT[:, rows - 1:rows], carry_ref.shape)
    c_rows = jnp.concatenate([cT, jnp.zeros((LANES - B_HEADS, rows), f32)], axis=0).T
    kb16_ref[0] = _bias_columns(c_rows * (-LOG2E))


def _proj_prompt(x, g, w16, wf16, bfp, lng, lnb, ws, bsT):
    bsz, seq, _ = x.shape
    rows = PROJ_ROWS
    const = lambda *shape: pl.BlockSpec(shape, lambda b, i: (0,) * len(shape))
    tile_rows = lambda width: pl.BlockSpec((1, rows, width), lambda b, i: (b, i, 0))
    tile_cols = lambda height: pl.BlockSpec((1, height, rows), lambda b, i: (b, 0, i))
    out_shape = (
        jax.ShapeDtypeStruct((bsz, B_WIDTH, seq), f32),
        jax.ShapeDtypeStruct((bsz, B_WIDTH, seq), f32),
        jax.ShapeDtypeStruct((bsz, B_HEADS, seq), f32),
        jax.ShapeDtypeStruct((bsz, B_WIDTH, seq), bf16),
        jax.ShapeDtypeStruct((bsz, seq, B_WIDTH), bf16),
        jax.ShapeDtypeStruct((bsz, seq, LANES), bf16),
        jax.ShapeDtypeStruct((bsz, B_WIDTH, seq), bf16),
        jax.ShapeDtypeStruct((bsz, seq, A_WIDTH), bf16),
        jax.ShapeDtypeStruct((bsz, seq, B_WIDTH), f32),
    )
    out_specs = (tile_cols(B_WIDTH), tile_cols(B_WIDTH), tile_cols(B_HEADS), tile_cols(B_WIDTH),
                 tile_rows(B_WIDTH), tile_rows(LANES), tile_cols(B_WIDTH), tile_rows(A_WIDTH), tile_rows(B_WIDTH))
    return pl.pallas_call(
        _proj_prompt_kernel,
        grid=(bsz, seq // rows),
        in_specs=[tile_rows(D_MODEL), const(1, D_MODEL), const(D_MODEL, MAIN_WIDTH), const(D_MODEL, LANES),
                  const(1, LANES), const(1, A_WIDTH), const(1, A_WIDTH), const(A_GROUPS, CHUNK, CHUNK),
                  const(CHUNK, A_GROUPS)],
        out_specs=out_specs,
        out_shape=out_shape,
        scratch_shapes=[pltpu.VMEM((B_HEADS, LANES), f32)],
        compiler_params=pltpu.CompilerParams(dimension_semantics=("arbitrary", "arbitrary"),
                                             vmem_limit_bytes=VMEM_LIMIT),
        name="proj_prompt",
    )(x, g, w16, wf16, bfp, lng, lnb, ws, bsT)


class _FlashOps:
    def __init__(self, kp_ref, kb_ref, qT_ref, vT_ref, o_ref, qaug_ref, acc_ref, m_ref, h):
        self.kp, self.kb, self.qT, self.vT, self.o = kp_ref, kb_ref, qT_ref, vT_ref, o_ref
        self.qaug, self.acc, self.m = qaug_ref, acc_ref, m_ref
        t, ks = Q_TILE, K_SUB
        self.par = h % 2
        self.row = lax.broadcasted_iota(jnp.int32, (LANES, t), 0)
        self.ones_rows = jnp.where((self.row == h) | (self.row == h + B_HEADS) | (self.row == h + 2 * B_HEADS),
                                   1.0, 0.0).astype(bf16)
        self.causal = lax.broadcasted_iota(jnp.int32, (ks, t), 0) <= lax.broadcasted_iota(jnp.int32, (ks, t), 1)
        self.sum_rows = jnp.ones((SUM_ROWS, ks), bf16)

    def scores(self, k0, lo):
        kt = jnp.concatenate([self.kp[0, pl.ds(k0, K_SUB), :], self.kb[0, pl.ds(k0, K_SUB), :]], axis=1)
        return jnp.dot(kt, self.qaug[:, lo:], preferred_element_type=f32)

    def consume(self, s, k0, lo, masked):
        if masked:
            s = jnp.where(self.causal[:, :Q_TILE - lo], s, NEG_BIG)
        m_old = self.m[:, lo:]
        m_new = jnp.maximum(m_old, jnp.max(s, axis=0, keepdims=True))
        p = jnp.exp2(s - m_new).astype(bf16)
        alpha = jnp.exp2(m_old - m_new)
        v_aug = jnp.concatenate([self.vT[0, :, pl.ds(k0, K_SUB)], self.sum_rows], axis=0)
        self.acc[:, lo:] = alpha * self.acc[:, lo:] + jnp.dot(v_aug, p, preferred_element_type=f32)
        self.m[:, lo:] = m_new

    def init_tile(self, qi):
        q0 = pl.multiple_of(qi * Q_TILE, Q_TILE)
        qp = self.qT[0, :, pl.ds(q0, Q_TILE)].astype(f32)
        self.qaug[0:LANES, :] = jnp.where((self.row // B_HEAD_DIM) == self.par, qp, 0.0).astype(bf16)
        self.qaug[LANES:2 * LANES, :] = self.ones_rows
        self.m[...] = jnp.full_like(self.m, NEG_BIG)
        self.acc[...] = jnp.zeros_like(self.acc)

    def block_stages(self, k_base, diagonal):
        nsub = Q_TILE // K_SUB
        ahead = []
        lo = lambda r: r * K_SUB if diagonal else 0
        k_of = lambda r: pl.multiple_of(k_base + r * K_SUB, K_SUB)

        def first():
            for r in range(min(LOOKAHEAD, nsub)):
                ahead.append(self.scores(k_of(r), lo(r)))

        def stage(r):
            if r + LOOKAHEAD < nsub:
                ahead.append(self.scores(k_of(r + LOOKAHEAD), lo(r + LOOKAHEAD)))
            self.consume(ahead.pop(0), k_of(r), lo(r), diagonal)

        return [first] + [functools.partial(stage, r) for r in range(nsub)]

    def finalize(self, qi):
        q0 = pl.multiple_of(qi * Q_TILE, Q_TILE)
        self.o[0, :, pl.ds(q0, Q_TILE)] = self.acc[0:B_HEAD_DIM, :] / self.acc[B_HEAD_DIM:B_HEAD_DIM + 1, :]


def _flash_kernel(h0, kp_ref, kb_ref, qT_ref, vT_ref, o_prev_ref, o_ref, qaug_ref, acc_ref, m_ref):
    del o_prev_ref
    ops = _FlashOps(kp_ref, kb_ref, qT_ref, vT_ref, o_ref, qaug_ref, acc_ref, m_ref, h0 + pl.program_id(0))

    def q_body(qi, carry):
        ops.init_tile(qi)

        def kv_body(kj, c):
            for stage in ops.block_stages(pl.multiple_of(kj * Q_TILE, Q_TILE), False):
                stage()
            return c

        lax.fori_loop(0, qi, kv_body, 0)
        for stage in ops.block_stages(pl.multiple_of(qi * Q_TILE, Q_TILE), True):
            stage()
        ops.finalize(qi)
        return carry

    lax.fori_loop(0, kp_ref.shape[1] // Q_TILE, q_body, 0)


def _flash_heads(k16, kb16, qT16, vT16, oT, b0, h0, nh):
    seq = k16.shape[1]
    t = Q_TILE
    return pl.pallas_call(
        functools.partial(_flash_kernel, h0),
        grid=(nh,),
        in_specs=[pl.BlockSpec((1, seq, LANES), lambda h: (b0, 0, (h0 + h) // 2)),
                  pl.BlockSpec((1, seq, LANES), lambda h: (b0, 0, 0)),
                  pl.BlockSpec((1, LANES, seq), lambda h: (b0, (h0 + h) // 2, 0)),
                  pl.BlockSpec((1, B_HEAD_DIM, seq), lambda h: (b0, h0 + h, 0)),
                  pl.BlockSpec(memory_space=pl.ANY)],
        out_specs=pl.BlockSpec((1, B_HEAD_DIM, seq), lambda h: (b0, h0 + h, 0)),
        out_shape=jax.ShapeDtypeStruct(oT.shape, f32),
        input_output_aliases={4: 0},
        scratch_shapes=[pltpu.VMEM((2 * LANES, t), bf16), pltpu.VMEM((B_HEAD_DIM + SUM_ROWS, t), f32),
                        pltpu.VMEM((1, t), f32)],
        compiler_params=pltpu.CompilerParams(dimension_semantics=("arbitrary",), vmem_limit_bytes=VMEM_LIMIT),
        name="flash_heads",
    )(k16, kb16, qT16, vT16, oT)


def _proj_sample_kernel(x_ref, g_ref, w_ref, wf_ref, bf_ref, lng_ref, lnb_ref, ws_ref, bsT_ref,
                        k_ref, v_ref, lf_ref, vn_ref, q_ref, kaug_ref, v16_ref, ga_ref, gzb_ref):
    rows = x_ref.shape[0]
    t = 8
    ua, vn, za, q, k, v, zb, logf = _project(x_ref[...], g_ref, w_ref, wf_ref, bf_ref, lng_ref, lnb_ref)

    r_i = lax.broadcasted_iota(jnp.int32, (rows, rows), 0)
    c_i = lax.broadcasted_iota(jnp.int32, (rows, rows), 1)
    block = ((r_i // t) == (c_i // t)) & ((c_i % t) <= (r_i % t))
    spread = (lax.broadcasted_iota(jnp.int32, (LANES, rows), 0)
              == lax.broadcasted_iota(jnp.int32, (LANES, rows), 1) % t)
    spread = jnp.where(spread, 1.0, 0.0).astype(bf16)
    mixed = []
    for g in range(A_GROUPS):
        w8 = jnp.dot(ws_ref[g, 0:t, :].astype(bf16), spread, preferred_element_type=f32)
        wt = jnp.broadcast_to(w8[None], (rows // t, t, rows)).reshape(rows, rows)
        wbd = jnp.where(block, wt, 0.0).astype(bf16)
        b8 = jnp.broadcast_to(bsT_ref[0:t, g:g + 1], (t, CHUNK))
        bcol = jnp.broadcast_to(b8[None], (rows // t, t, CHUNK)).reshape(rows, CHUNK)
        mixed.append(jnp.dot(wbd, vn[g].astype(bf16), preferred_element_type=f32) + bcol)
    mix = jnp.concatenate(mixed, axis=1)
    ga_ref[...] = (ua * mix * jax.nn.silu(za)).astype(bf16)
    gzb_ref[...] = jax.nn.silu(zb)
    vn_ref[...] = jnp.concatenate(vn, axis=1)
    k_ref[...] = k
    v_ref[...] = v
    v16_ref[...] = v.astype(bf16)
    lf_ref[...] = logf
    q_ref[...] = q * (QK_SCALE * LOG2E)

    lf3 = logf.reshape(rows // t, t, LANES)
    tok = lax.broadcasted_iota(jnp.int32, lf3.shape, 1)
    c3 = jnp.zeros_like(lf3)
    for s in range(t):
        c3 = c3 + jnp.where(tok >= s, jnp.broadcast_to(lf3[:, s:s + 1, :], lf3.shape), 0.0)
    c_rows = c3.reshape(rows, LANES)
    kaug_ref[:, 0:B_WIDTH] = k.astype(bf16)
    kaug_ref[:, B_WIDTH:B_WIDTH + LANES] = _bias_columns(c_rows * (-LOG2E))


def _proj_sample(x, g, w16, wf16, bfp, lng, lnb, ws, bsT):
    n = x.shape[0]
    rows = PROJ_ROWS
    const = lambda *shape: pl.BlockSpec(shape, lambda i: (0,) * len(shape))
    tile = lambda width: pl.BlockSpec((rows, width), lambda i: (i, 0))
    out_shape = (
        jax.ShapeDtypeStruct((n, B_WIDTH), f32),
        jax.ShapeDtypeStruct((n, B_WIDTH), f32),
        jax.ShapeDtypeStruct((n, LANES), f32),
        jax.ShapeDtypeStruct((n, A_WIDTH), f32),
        jax.ShapeDtypeStruct((n, B_WIDTH), f32),
        jax.ShapeDtypeStruct((n, B_WIDTH + LANES), bf16),
        jax.ShapeDtypeStruct((n, B_WIDTH), bf16),
        jax.ShapeDtypeStruct((n, A_WIDTH), bf16),
        jax.ShapeDtypeStruct((n, B_WIDTH), f32),
    )
    out_specs = (tile(B_WIDTH), tile(B_WIDTH), tile(LANES), tile(A_WIDTH), tile(B_WIDTH),
                 tile(B_WIDTH + LANES), tile(B_WIDTH), tile(A_WIDTH), tile(B_WIDTH))
    return pl.pallas_call(
        _proj_sample_kernel,
        grid=(n // rows,),
        in_specs=[tile(D_MODEL), const(1, D_MODEL), const(D_MODEL, MAIN_WIDTH), const(D_MODEL, LANES),
                  const(1, LANES), const(1, A_WIDTH), const(1, A_WIDTH), const(A_GROUPS, CHUNK, CHUNK),
                  const(CHUNK, A_GROUPS)],
        out_specs=out_specs,
        out_shape=out_shape,
        compiler_params=pltpu.CompilerParams(dimension_semantics=("arbitrary",),
                                             vmem_limit_bytes=VMEM_LIMIT),
        name="proj_sample",
    )(x, g, w16, wf16, bfp, lng, lnb, ws, bsT)


def _past_decay_kernel(pt_ref, *refs):
    f_ref, o_ref = refs
    b = pl.program_id(0)
    n = o_ref.shape[2] // PAGE_SIZE
    lf = jnp.stack([f_ref[pt_ref[b, p]] for p in range(n)], axis=0)
    r_i = lax.broadcasted_iota(jnp.int32, (PAGE_SIZE, 2 * PAGE_SIZE), 0)
    c_i = lax.broadcasted_iota(jnp.int32, (PAGE_SIZE, 2 * PAGE_SIZE), 1)
    tri = jnp.where((r_i >= c_i) | (c_i >= PAGE_SIZE), 1.0, 0.0).astype(bf16)
    sums = sum(jnp.dot(part.reshape(n * B_HEADS, PAGE_SIZE).astype(bf16), tri, preferred_element_type=f32)
               for part in _split3_bf16(lf))
    incl = sums[:, :PAGE_SIZE].reshape(lf.shape)
    later = sums[:, PAGE_SIZE:].reshape(lf.shape)
    later = jnp.concatenate([later[1:], jnp.zeros_like(later[:1])], axis=0)
    sh = 1
    while sh < n:
        later = later + jnp.concatenate([later[sh:], jnp.zeros_like(later[:sh])], axis=0)
        sh *= 2
    d = (incl - lf + later) * LOG2E
    for p in range(n):
        o_ref[0, :, p * PAGE_SIZE:(p + 1) * PAGE_SIZE] = d[p]


def _past_decay(page_table, lf_cache):
    nb, n_pages = page_table.shape

    grid_spec = pltpu.PrefetchScalarGridSpec(
        num_scalar_prefetch=1,
        grid=(nb,),
        in_specs=[pl.BlockSpec(memory_space=pltpu.VMEM)],
        out_specs=pl.BlockSpec((1, B_HEADS, n_pages * PAGE_SIZE), lambda b, pt: (b, 0, 0)),
    )
    return pl.pallas_call(
        _past_decay_kernel,
        grid_spec=grid_spec,
        out_shape=jax.ShapeDtypeStruct((nb, B_HEADS, n_pages * PAGE_SIZE), f32),
        compiler_params=pltpu.CompilerParams(dimension_semantics=("arbitrary",), vmem_limit_bytes=VMEM_LIMIT),
        name="past_decay",
    )(page_table, lf_cache)


class _DecodeOps:
    TOKENS = 8

    def __init__(self, k_refs, v_refs, d_ref, q_ref, kaug_ref, v16_ref, o_ref, qbd_ref, acc_ref, m_ref, l_ref):
        self.k_refs, self.v_refs, self.d, self.q, self.kaug, self.v16, self.o = (
            k_refs, v_refs, d_ref, q_ref, kaug_ref, v16_ref, o_ref)
        self.qbd, self.acc, self.m, self.l = qbd_ref, acc_ref, m_ref, l_ref

    def init(self):
        t = self.TOKENS
        rows = B_HEADS * t
        qrep = jnp.concatenate([self.q[0]] * B_HEADS, axis=0)
        r_i = lax.broadcasted_iota(jnp.int32, (rows, B_WIDTH), 0)
        c_i = lax.broadcasted_iota(jnp.int32, (rows, B_WIDTH), 1)
        self.qbd[:, 0:B_WIDTH] = jnp.where((r_i // t) == (c_i // B_HEAD_DIM), qrep, 0.0).astype(bf16)
        r_b = lax.broadcasted_iota(jnp.int32, (rows, LANES), 0) // t
        c_b = lax.broadcasted_iota(jnp.int32, (rows, LANES), 1)
        ones = (c_b == r_b) | (c_b == r_b + B_HEADS) | (c_b == r_b + 2 * B_HEADS)
        self.qbd[:, B_WIDTH:B_WIDTH + LANES] = jnp.where(ones, 1.0, 0.0).astype(bf16)
        self.m[...] = jnp.full_like(self.m, NEG_BIG)
        self.l[...] = jnp.zeros_like(self.l)
        self.acc[...] = jnp.zeros_like(self.acc)

    def update(self, s, vals, contract_lanes):
        m_old = self.m[...]
        m_new = jnp.maximum(m_old, jnp.max(s, axis=1, keepdims=True))
        p = jnp.exp2(s - m_new)
        alpha = jnp.exp2(m_old - m_new)
        self.l[...] = alpha * self.l[...] + jnp.sum(p, axis=1, keepdims=True)
        if contract_lanes:
            pv = lax.dot_general(p.astype(bf16), vals, (((1,), (1,)), ((), ())), preferred_element_type=f32)
        else:
            pv = jnp.dot(p.astype(bf16), vals, preferred_element_type=f32)
        self.acc[...] = alpha * self.acc[...] + pv
        self.m[...] = m_new

    def stages(self):
        t = self.TOKENS
        npg = len(self.k_refs)
        gp = npg // DECODE_GROUPS
        gw = gp * PAGE_SIZE
        scores = []
        pages16 = lambda refs, g: jnp.concatenate([r[0].astype(bf16) for r in refs[g * gp:(g + 1) * gp]], axis=1)

        def first():
            d = self.d[0]
            dfull = jnp.concatenate([jnp.broadcast_to(d[hh:hh + 1, :], (t, npg * PAGE_SIZE))
                                     for hh in range(B_HEADS)], axis=0)
            for g in range(DECODE_GROUPS):
                scores.append(jnp.dot(self.qbd[:, 0:B_WIDTH], pages16(self.k_refs, g), preferred_element_type=f32)
                              + dfull[:, g * gw:(g + 1) * gw])

        def stage(g):
            self.update(scores[g], pages16(self.v_refs, g), True)

        return [first] + [functools.partial(stage, g) for g in range(DECODE_GROUPS)]

    def finalize(self):
        t = self.TOKENS
        sn = lax.dot_general(self.qbd[...], self.kaug[0], (((1,), (1,)), ((), ())),
                             preferred_element_type=f32)
        tq = lax.broadcasted_iota(jnp.int32, sn.shape, 0) % t
        tk = lax.broadcasted_iota(jnp.int32, sn.shape, 1)
        self.update(jnp.where(tk <= tq, sn, NEG_BIG), self.v16[0], False)
        res = self.acc[...] / self.l[...]
        r_i = lax.broadcasted_iota(jnp.int32, (t, B_WIDTH), 1) // B_HEAD_DIM
        out = jnp.zeros((t, B_WIDTH), f32)
        for hh in range(B_HEADS):
            out = out + jnp.where(r_i == hh, res[hh * t:(hh + 1) * t, :], 0.0)
        self.o[0] = out


def _interleave(a, b):
    out = []
    for i in range(max(len(a), len(b))):
        out += a[i:i + 1] + b[i:i + 1]
    return out


def _attend_kernel(steps_per_seq, pt_ref, work_ref, *refs):
    npg = PAGES_PER_STEP
    k_refs, v_refs = refs[0:npg], refs[npg:2 * npg]
    (d_ref, q_ref, kaug_ref, v16_ref, kp_ref, kb_ref, qT_ref, vT_ref, o_dec_ref, o_fl_ref,
     qbd_ref, dacc_ref, dm_ref, dl_ref, qaug_ref, facc_ref, fm_ref) = refs[2 * npg:]
    g = pl.program_id(0)
    j = g % steps_per_seq
    h, qi, kj = work_ref[1, g], work_ref[2, g], work_ref[3, g]
    dec = _DecodeOps(k_refs, v_refs, d_ref, q_ref, kaug_ref, v16_ref, o_dec_ref, qbd_ref, dacc_ref, dm_ref, dl_ref)
    fl = _FlashOps(kp_ref, kb_ref, qT_ref, vT_ref, o_fl_ref, qaug_ref, facc_ref, fm_ref, h)

    def run(stages):
        for stage in stages:
            stage()

    pl.when(j == 0)(dec.init)
    pl.when(kj == 0)(lambda: fl.init_tile(qi))

    @pl.when((kj >= 0) & (kj < qi))
    def _():
        run(_interleave(dec.stages(), fl.block_stages(pl.multiple_of(kj * Q_TILE, Q_TILE), False)))

    @pl.when(kj == qi)
    def _():
        run(_interleave(dec.stages(), fl.block_stages(pl.multiple_of(qi * Q_TILE, Q_TILE), True)))
        fl.finalize(qi)

    pl.when(kj < 0)(lambda: run(dec.stages()))
    pl.when(j == steps_per_seq - 1)(dec.finalize)


def _attend(page_table, kT_cache, vT_cache, decay, q, kaug, v16, k16, kb16, qT16, vT16):
    nb, n_pages = page_table.shape
    bsz, seq, _ = k16.shape
    npg = PAGES_PER_STEP
    steps_per_seq = n_pages // npg
    steps = nb * steps_per_seq
    t = _DecodeOps.TOKENS
    nq = seq // Q_TILE
    items = [(qi, kj) for qi in range(nq) for kj in range(qi + 1)]
    pairs = min(bsz * B_HEADS, steps // len(items))
    work = [(p // B_HEADS, p % B_HEADS, qi, kj) for p in range(pairs) for qi, kj in items]
    work += [((pairs - 1) // B_HEADS, (pairs - 1) % B_HEADS, 0, -1)] * (steps - len(work))
    work = jnp.asarray(work, jnp.int32).T

    def page_map(i):
        return lambda g, pt, wk: (pt[g // steps_per_seq, (g % steps_per_seq) * npg + i], 0, 0)

    per_seq = lambda width: pl.BlockSpec((1, t, width), lambda g, pt, wk: (g // steps_per_seq, 0, 0))
    page_spec = lambda i: pl.BlockSpec((1, B_WIDTH, PAGE_SIZE), page_map(i))
    in_specs = ([page_spec(i) for i in range(npg)] + [page_spec(i) for i in range(npg)]
                + [pl.BlockSpec((1, B_HEADS, npg * PAGE_SIZE),
                                lambda g, pt, wk: (g // steps_per_seq, 0, g % steps_per_seq))]
                + [per_seq(B_WIDTH), per_seq(B_WIDTH + LANES), per_seq(B_WIDTH)]
                + [pl.BlockSpec((1, seq, LANES), lambda g, pt, wk: (wk[0, g], 0, wk[1, g] // 2)),
                   pl.BlockSpec((1, seq, LANES), lambda g, pt, wk: (wk[0, g], 0, 0)),
                   pl.BlockSpec((1, LANES, seq), lambda g, pt, wk: (wk[0, g], wk[1, g] // 2, 0)),
                   pl.BlockSpec((1, B_HEAD_DIM, seq), lambda g, pt, wk: (wk[0, g], wk[1, g], 0))])
    grid_spec = pltpu.PrefetchScalarGridSpec(
        num_scalar_prefetch=2,
        grid=(steps,),
        in_specs=in_specs,
        out_specs=(per_seq(B_WIDTH),
                   pl.BlockSpec((1, B_HEAD_DIM, seq), lambda g, pt, wk: (wk[0, g], wk[1, g], 0))),
        scratch_shapes=[pltpu.VMEM((B_HEADS * t, B_WIDTH + LANES), bf16),
                        pltpu.VMEM((B_HEADS * t, B_WIDTH), f32),
                        pltpu.VMEM((B_HEADS * t, 1), f32), pltpu.VMEM((B_HEADS * t, 1), f32),
                        pltpu.VMEM((2 * LANES, Q_TILE), bf16), pltpu.VMEM((B_HEAD_DIM + SUM_ROWS, Q_TILE), f32),
                        pltpu.VMEM((1, Q_TILE), f32)],
    )
    o_dec, oT = pl.pallas_call(
        functools.partial(_attend_kernel, steps_per_seq),
        grid_spec=grid_spec,
        out_shape=(jax.ShapeDtypeStruct((nb, t, B_WIDTH), f32), jax.ShapeDtypeStruct((bsz, B_WIDTH, seq), f32)),
        compiler_params=pltpu.CompilerParams(dimension_semantics=("arbitrary",), vmem_limit_bytes=VMEM_LIMIT),
        name="attend",
    )(page_table, work, *([kT_cache] * npg), *([vT_cache] * npg), decay,
      q.reshape(nb, t, B_WIDTH), kaug.reshape(nb, t, B_WIDTH + LANES), v16.reshape(nb, t, B_WIDTH),
      k16, kb16, qT16, vT16)
    return o_dec, oT, pairs


def _finish_kernel(transposed, x_ref, ga_ref, o_ref, gzb_ref, p_ref, wo_ref, gpost_ref, wpg_ref, bpg_ref,
                   wpe_ref, y_ref):
    batched = x_ref.ndim == 3
    ld = (lambda r: r[0]) if batched else (lambda r: r[...])
    x = ld(x_ref)
    o = ld(o_ref)
    if transposed:
        o = o.T
    mix_b = (o * ld(gzb_ref)).astype(bf16)
    y = (jnp.dot(ld(ga_ref), wo_ref[0:A_WIDTH, :], preferred_element_type=f32)
         + jnp.dot(mix_b, wo_ref[A_WIDTH:, :], preferred_element_type=f32))
    ms = jnp.mean(y * y, axis=-1, keepdims=True)
    x = x + y * lax.rsqrt(ms + RMS_EPS) * gpost_ref[...]
    gate = jax.nn.sigmoid(jnp.dot(x.astype(bf16), wpg_ref[...], preferred_element_type=f32) + bpg_ref[...])
    res = x + gate * jnp.dot(ld(p_ref).astype(bf16), wpe_ref[...], preferred_element_type=f32)
    if batched:
        y_ref[0] = res
    else:
        y_ref[...] = res


def _finish_prompt(x, ga, oT, gzb, p, wo16, gpost, wpg16, bpg, wpe16):
    bsz, seq, _ = x.shape
    rows = PROJ_ROWS
    const = lambda *shape: pl.BlockSpec(shape, lambda b, i: (0,) * len(shape))
    tile_rows = lambda width: pl.BlockSpec((1, rows, width), lambda b, i: (b, i, 0))
    return pl.pallas_call(
        functools.partial(_finish_kernel, True),
        grid=(bsz, seq // rows),
        in_specs=[tile_rows(D_MODEL), tile_rows(A_WIDTH),
                  pl.BlockSpec((1, B_WIDTH, rows), lambda b, i: (b, 0, i)),
                  tile_rows(B_WIDTH), tile_rows(PLE_DIM), const(D_MODEL, D_MODEL), const(1, D_MODEL),
                  const(D_MODEL, D_MODEL), const(1, D_MODEL), const(PLE_DIM, D_MODEL)],
        out_specs=tile_rows(D_MODEL),
        out_shape=jax.ShapeDtypeStruct(x.shape, f32),
        compiler_params=pltpu.CompilerParams(dimension_semantics=("arbitrary", "arbitrary"),
                                             vmem_limit_bytes=VMEM_LIMIT),
        name="finish_prompt",
    )(x, ga, oT, gzb, p, wo16, gpost, wpg16, bpg, wpe16)


def _finish_sample(x, ga, o, gzb, p, wo16, gpost, wpg16, bpg, wpe16):
    n = x.shape[0]
    rows = PROJ_ROWS
    const = lambda *shape: pl.BlockSpec(shape, lambda i: (0,) * len(shape))
    tile = lambda width: pl.BlockSpec((rows, width), lambda i: (i, 0))
    return pl.pallas_call(
        functools.partial(_finish_kernel, False),
        grid=(n // rows,),
        in_specs=[tile(D_MODEL), tile(A_WIDTH), tile(B_WIDTH), tile(B_WIDTH), tile(PLE_DIM),
                  const(D_MODEL, D_MODEL), const(1, D_MODEL), const(D_MODEL, D_MODEL), const(1, D_MODEL),
                  const(PLE_DIM, D_MODEL)],
        out_specs=tile(D_MODEL),
        out_shape=jax.ShapeDtypeStruct(x.shape, f32),
        compiler_params=pltpu.CompilerParams(dimension_semantics=("arbitrary",),
                                             vmem_limit_bytes=VMEM_LIMIT),
        name="finish_sample",
    )(x, ga, o, gzb, p, wo16, gpost, wpg16, bpg, wpe16)


def kernel(x_prompt, x_sample, cache_k, cache_v, cache_logf, page_table, p_prompt, p_sample, ln_pre_g, w_in, b_f,
           ln_v_g, ln_v_b, w_s, b_s, w_out, ln_post_g, w_pe, w_pg, b_pg):
    assert w_in.shape[0] == 1, "single-layer step"
    bsz, seq, _ = x_prompt.shape
    nb, t, _ = x_sample.shape
    n_pool = cache_k.shape[1]

    w16 = w_in[0, :, :MAIN_WIDTH].astype(bf16)
    wf16 = jnp.pad(w_in[0, :, MAIN_WIDTH:], ((0, 0), (0, LANES - B_HEADS))).astype(bf16)
    bfp = jnp.pad(b_f[0], (0, LANES - B_HEADS)).reshape(1, LANES)
    g_pre = ln_pre_g[0].reshape(1, D_MODEL)
    lng = ln_v_g[0].reshape(1, A_WIDTH)
    lnb = ln_v_b[0].reshape(1, A_WIDTH)
    bsT = b_s[0].T
    wo16 = w_out[0].astype(bf16)
    wpg16 = w_pg[0].astype(bf16)
    wpe16 = w_pe[0].astype(bf16)
    gpost = ln_post_g[0].reshape(1, D_MODEL)
    bpg = b_pg[0].reshape(1, D_MODEL)

    kT, vT, lfT, qT16, k16, kb16, vT16, ga, gzb = _proj_prompt(x_prompt, g_pre, w16, wf16, bfp, lng, lnb, w_s[0], bsT)
    new_k_prompt = kT.reshape(bsz, B_HEADS, B_HEAD_DIM, seq).transpose(0, 3, 1, 2)[None]
    new_v_prompt = vT.reshape(bsz, B_HEADS, B_HEAD_DIM, seq).transpose(0, 3, 1, 2)[None]
    new_logf_prompt = lfT.transpose(0, 2, 1)[None]

    kT_cache = cache_k[0].transpose(0, 2, 3, 1).reshape(n_pool, B_WIDTH, PAGE_SIZE)
    vT_cache = cache_v[0].transpose(0, 2, 3, 1).reshape(n_pool, B_WIDTH, PAGE_SIZE)
    lf_cache = cache_logf[0].transpose(0, 2, 1)
    xs = x_sample.reshape(nb * t, D_MODEL)
    ks, vs, lfs, vns, qs, kaug, vs16, gas, gzbs = _proj_sample(xs, g_pre, w16, wf16, bfp, lng, lnb, w_s[0], bsT)
    decay = _past_decay(page_table, lf_cache)
    o_s, oT, pairs = _attend(page_table, kT_cache, vT_cache, decay, qs, kaug, vs16, k16, kb16, qT16, vT16)
    for b0 in range(pairs // B_HEADS, bsz):
        h0 = pairs % B_HEADS if b0 == pairs // B_HEADS else 0
        oT = _flash_heads(k16, kb16, qT16, vT16, oT, b0, h0, B_HEADS - h0)
    y_prompt = _finish_prompt(x_prompt, ga, oT, gzb, p_prompt[0], wo16, gpost, wpg16, bpg, wpe16)
    y_sample = _finish_sample(xs, gas, o_s.reshape(nb * t, B_WIDTH), gzbs, p_sample[0].reshape(nb * t, PLE_DIM),
                              wo16, gpost, wpg16, bpg, wpe16).reshape(nb, t, D_MODEL)

    return (y_prompt, y_sample, new_k_prompt, new_v_prompt, new_logf_prompt,
            ks.reshape(1, nb, t, B_HEADS, B_HEAD_DIM), vs.reshape(1, nb, t, B_HEADS, B_HEAD_DIM),
            lfs[:, :B_HEADS].reshape(1, nb, t, B_HEADS), vns.reshape(1, nb, t, A_GROUPS, CHUNK))
```

```python
import functools
import math

import jax
import jax.numpy as jnp
from jax import lax
from jax.experimental import pallas as pl
from jax.experimental.pallas import tpu as pltpu

D_MODEL = 1024
A_WIDTH = 512
B_WIDTH = 512
CHUNK = 128
A_GROUPS = 4
B_HEADS = 8
B_HEAD_DIM = 64
PLE_DIM = 256
PAGE_SIZE = 128
RMS_EPS = 1e-6
LN_EPS = 1e-5
MAIN_WIDTH = 3 * A_WIDTH + 4 * B_WIDTH
LANES = 128
LOG2E = math.log2(math.e)
QK_SCALE = B_HEAD_DIM ** -0.5
NEG_BIG = -1e30

PROJ_ROWS = 512
Q_TILE = 1024
K_SUB = 256
SUM_ROWS = 16
LOOKAHEAD = 3
PAGES_PER_STEP = 16
DECODE_GROUPS = 4
VMEM_LIMIT = 56 * 1024 * 1024

f32 = jnp.float32
bf16 = jnp.bfloat16


def _log_sigmoid(x):
    return jnp.minimum(x, 0.0) - jnp.log1p(jnp.exp(-jnp.abs(x)))


def _split3_bf16(x):
    hi = x.astype(bf16).astype(f32)
    r = x - hi
    mid = r.astype(bf16).astype(f32)
    lo = (r - mid).astype(bf16).astype(f32)
    return hi, mid, lo


def _bias_columns(nb):
    hi, mid, lo = _split3_bf16(nb)
    packed = hi + pltpu.roll(mid, B_HEADS, axis=1) + pltpu.roll(lo, 2 * B_HEADS, axis=1)
    return packed.astype(bf16)


def _project(x, g_ref, w_ref, wf_ref, bf_ref, lng_ref, lnb_ref):
    ms = jnp.mean(x * x, axis=-1, keepdims=True)
    h = (x * lax.rsqrt(ms + RMS_EPS) * g_ref[...]).astype(bf16)

    def sec(i):
        return jnp.dot(h, w_ref[:, i * 512:(i + 1) * 512], preferred_element_type=f32)

    ua = jax.nn.gelu(sec(0))
    va = jax.nn.gelu(sec(1))
    za = sec(2)
    q = sec(3)
    k = sec(4)
    v = sec(5)
    zb = sec(6)
    f = jnp.dot(h, wf_ref[...], preferred_element_type=f32) + bf_ref[...]
    col = lax.broadcasted_iota(jnp.int32, f.shape, 1)
    logf = jnp.where(col < B_HEADS, _log_sigmoid(f), 0.0)
    vn = []
    for g in range(A_GROUPS):
        vg = va[:, g * CHUNK:(g + 1) * CHUNK]
        mu = jnp.mean(vg, axis=-1, keepdims=True)
        var = jnp.mean(jnp.square(vg - mu), axis=-1, keepdims=True)
        vn.append((vg - mu) * lax.rsqrt(var + LN_EPS) * lng_ref[:, g * CHUNK:(g + 1) * CHUNK]
                  + lnb_ref[:, g * CHUNK:(g + 1) * CHUNK])
    return ua, vn, za, q, k, v, zb, logf


def _lane_cumsum(x):
    n = x.shape[-1]
    idx = lax.broadcasted_iota(jnp.int32, x.shape, x.ndim - 1)
    sh = 1
    while sh < n:
        x = x + jnp.where(idx >= sh, pltpu.roll(x, sh, axis=x.ndim - 1), 0.0)
        sh *= 2
    return x


def _lane_suffix_sum(x):
    n = x.shape[-1]
    idx = lax.broadcasted_iota(jnp.int32, x.shape, x.ndim - 1)
    sh = 1
    while sh < n:
        x = x + jnp.where(idx < n - sh, pltpu.roll(x, n - sh, axis=x.ndim - 1), 0.0)
        sh *= 2
    return x


def _proj_prompt_kernel(x_ref, g_ref, w_ref, wf_ref, bf_ref, lng_ref, lnb_ref, ws_ref, bsT_ref,
                        kT_ref, vT_ref, lfT_ref, qT16_ref, k16_ref, kb16_ref, vT16_ref, ga_ref, gzb_ref,
                        carry_ref):
    rows = x_ref.shape[1]

    @pl.when(pl.program_id(1) == 0)
    def _():
        carry_ref[...] = jnp.zeros_like(carry_ref)

    ua, vn, za, q, k, v, zb, logf = _project(x_ref[0], g_ref, w_ref, wf_ref, bf_ref, lng_ref, lnb_ref)

    tri = (lax.broadcasted_iota(jnp.int32, (CHUNK, CHUNK), 0)
           >= lax.broadcasted_iota(jnp.int32, (CHUNK, CHUNK), 1))
    mixed = []
    for g in range(A_GROUPS):
        wm = jnp.where(tri, ws_ref[g], 0.0).astype(bf16)
        bcol = jnp.broadcast_to(bsT_ref[:, g:g + 1], (CHUNK, CHUNK))
        vg = vn[g].astype(bf16)
        parts = [jnp.dot(wm, vg[c * CHUNK:(c + 1) * CHUNK], preferred_element_type=f32) + bcol
                 for c in range(rows // CHUNK)]
        mixed.append(jnp.concatenate(parts, axis=0))
    mix = jnp.concatenate(mixed, axis=1)
    ga_ref[0] = (ua * mix * jax.nn.silu(za)).astype(bf16)
    gzb_ref[0] = jax.nn.silu(zb)

    kT = k.T
    vT = v.T
    kT_ref[0] = kT
    vT_ref[0] = vT
    vT16_ref[0] = vT.astype(bf16)
    k16_ref[0] = k.astype(bf16)
    qT16_ref[0] = (q * (QK_SCALE * LOG2E)).T.astype(bf16)

    lfT = logf.T[0:B_HEADS, :]
    lfT_ref[0] = lfT
    cT = _lane_cumsum(lfT) + carry_ref[:, 0:1]
    carry_ref[...] = jnp.broadcast_to(cT[:, rows - 1:rows], carry_ref.shape)
    c_rows = jnp.concatenate([cT, jnp.zeros((LANES - B_HEADS, rows), f32)], axis=0).T
    kb16_ref[0] = _bias_columns(c_rows * (-LOG2E))


def _proj_prompt(x, g, w16, wf16, bfp, lng, lnb, ws, bsT):
    bsz, seq, _ = x.shape
    rows = PROJ_ROWS
    const = lambda *shape: pl.BlockSpec(shape, lambda b, i: (0,) * len(shape))
    tile_rows = lambda width: pl.BlockSpec((1, rows, width), lambda b, i: (b, i, 0))
    tile_cols = lambda height: pl.BlockSpec((1, height, rows), lambda b, i: (b, 0, i))
    out_shape = (
        jax.ShapeDtypeStruct((bsz, B_WIDTH, seq), f32),
        jax.ShapeDtypeStruct((bsz, B_WIDTH, seq), f32),
        jax.ShapeDtypeStruct((bsz, B_HEADS, seq), f32),
        jax.ShapeDtypeStruct((bsz, B_WIDTH, seq), bf16),
        jax.ShapeDtypeStruct((bsz, seq, B_WIDTH), bf16),
        jax.ShapeDtypeStruct((bsz, seq, LANES), bf16),
        jax.ShapeDtypeStruct((bsz, B_WIDTH, seq), bf16),
        jax.ShapeDtypeStruct((bsz, seq, A_WIDTH), bf16),
        jax.ShapeDtypeStruct((bsz, seq, B_WIDTH), f32),
    )
    out_specs = (tile_cols(B_WIDTH), tile_cols(B_WIDTH), tile_cols(B_HEADS), tile_cols(B_WIDTH),
                 tile_rows(B_WIDTH), tile_rows(LANES), tile_cols(B_WIDTH), tile_rows(A_WIDTH), tile_rows(B_WIDTH))
    return pl.pallas_call(
        _proj_prompt_kernel,
        grid=(bsz, seq // rows),
        in_specs=[tile_rows(D_MODEL), const(1, D_MODEL), const(D_MODEL, MAIN_WIDTH), const(D_MODEL, LANES),
                  const(1, LANES), const(1, A_WIDTH), const(1, A_WIDTH), const(A_GROUPS, CHUNK, CHUNK),
                  const(CHUNK, A_GROUPS)],
        out_specs=out_specs,
        out_shape=out_shape,
        scratch_shapes=[pltpu.VMEM((B_HEADS, LANES), f32)],
        compiler_params=pltpu.CompilerParams(dimension_semantics=("arbitrary", "arbitrary"),
                                             vmem_limit_bytes=VMEM_LIMIT),
        name="proj_prompt",
    )(x, g, w16, wf16, bfp, lng, lnb, ws, bsT)


class _FlashOps:
    def __init__(self, kp_ref, kb_ref, qT_ref, vT_ref, o_ref, qaug_ref, acc_ref, m_ref, h):
        self.kp, self.kb, self.qT, self.vT, self.o = kp_ref, kb_ref, qT_ref, vT_ref, o_ref
        self.qaug, self.acc, self.m = qaug_ref, acc_ref, m_ref
        t, ks = Q_TILE, K_SUB
        self.par = h % 2
        self.row = lax.broadcasted_iota(jnp.int32, (LANES, t), 0)
        self.ones_rows = jnp.where((self.row == h) | (self.row == h + B_HEADS) | (self.row == h + 2 * B_HEADS),
                                   1.0, 0.0).astype(bf16)
        self.causal = lax.broadcasted_iota(jnp.int32, (ks, t), 0) <= lax.broadcasted_iota(jnp.int32, (ks, t), 1)
        self.sum_rows = jnp.ones((SUM_ROWS, ks), bf16)

    def scores(self, k0, lo):
        kt = jnp.concatenate([self.kp[0, pl.ds(k0, K_SUB), :], self.kb[0, pl.ds(k0, K_SUB), :]], axis=1)
        return jnp.dot(kt, self.qaug[:, lo:], preferred_element_type=f32)

    def consume(self, s, k0, lo, masked):
        if masked:
            s = jnp.where(self.causal[:, :Q_TILE - lo], s, NEG_BIG)
        m_old = self.m[:, lo:]
        m_new = jnp.maximum(m_old, jnp.max(s, axis=0, keepdims=True))
        p = jnp.exp2(s - m_new).astype(bf16)
        alpha = jnp.exp2(m_old - m_new)
        v_aug = jnp.concatenate([self.vT[0, :, pl.ds(k0, K_SUB)], self.sum_rows], axis=0)
        self.acc[:, lo:] = alpha * self.acc[:, lo:] + jnp.dot(v_aug, p, preferred_element_type=f32)
        self.m[:, lo:] = m_new

    def init_tile(self, qi):
        q0 = pl.multiple_of(qi * Q_TILE, Q_TILE)
        qp = self.qT[0, :, pl.ds(q0, Q_TILE)].astype(f32)
        self.qaug[0:LANES, :] = jnp.where((self.row // B_HEAD_DIM) == self.par, qp, 0.0).astype(bf16)
        self.qaug[LANES:2 * LANES, :] = self.ones_rows
        self.m[...] = jnp.full_like(self.m, NEG_BIG)
        self.acc[...] = jnp.zeros_like(self.acc)

    def block_stages(self, k_base, diagonal):
        nsub = Q_TILE // K_SUB
        ahead = []
        lo = lambda r: r * K_SUB if diagonal else 0
        k_of = lambda r: pl.multiple_of(k_base + r * K_SUB, K_SUB)

        def first():
            for r in range(min(LOOKAHEAD, nsub)):
                ahead.append(self.scores(k_of(r), lo(r)))

        def stage(r):
            if r + LOOKAHEAD < nsub:
                ahead.append(self.scores(k_of(r + LOOKAHEAD), lo(r + LOOKAHEAD)))
            self.consume(ahead.pop(0), k_of(r), lo(r), diagonal)

        return [first] + [functools.partial(stage, r) for r in range(nsub)]

    def finalize(self, qi):
        q0 = pl.multiple_of(qi * Q_TILE, Q_TILE)
        self.o[0, :, pl.ds(q0, Q_TILE)] = self.acc[0:B_HEAD_DIM, :] / self.acc[B_HEAD_DIM:B_HEAD_DIM + 1, :]


def _flash_kernel(h0, kp_ref, kb_ref, qT_ref, vT_ref, o_prev_ref, o_ref, qaug_ref, acc_ref, m_ref):
    del o_prev_ref
    ops = _FlashOps(kp_ref, kb_ref, qT_ref, vT_ref, o_ref, qaug_ref, acc_ref, m_ref, h0 + pl.program_id(0))

    def q_body(qi, carry):
        ops.init_tile(qi)

        def kv_body(kj, c):
            for stage in ops.block_stages(pl.multiple_of(kj * Q_TILE, Q_TILE), False):
                stage()
            return c

        lax.fori_loop(0, qi, kv_body, 0)
        for stage in ops.block_stages(pl.multiple_of(qi * Q_TILE, Q_TILE), True):
            stage()
        ops.finalize(qi)
        return carry

    lax.fori_loop(0, kp_ref.shape[1] // Q_TILE, q_body, 0)


def _flash_heads(k16, kb16, qT16, vT16, oT, b0, h0, nh):
    seq = k16.shape[1]
    t = Q_TILE
    return pl.pallas_call(
        functools.partial(_flash_kernel, h0),
        grid=(nh,),
        in_specs=[pl.BlockSpec((1, seq, LANES), lambda h: (b0, 0, (h0 + h) // 2)),
                  pl.BlockSpec((1, seq, LANES), lambda h: (b0, 0, 0)),
                  pl.BlockSpec((1, LANES, seq), lambda h: (b0, (h0 + h) // 2, 0)),
                  pl.BlockSpec((1, B_HEAD_DIM, seq), lambda h: (b0, h0 + h, 0)),
                  pl.BlockSpec(memory_space=pl.ANY)],
        out_specs=pl.BlockSpec((1, B_HEAD_DIM, seq), lambda h: (b0, h0 + h, 0)),
        out_shape=jax.ShapeDtypeStruct(oT.shape, f32),
        input_output_aliases={4: 0},
        scratch_shapes=[pltpu.VMEM((2 * LANES, t), bf16), pltpu.VMEM((B_HEAD_DIM + SUM_ROWS, t), f32),
                        pltpu.VMEM((1, t), f32)],
        compiler_params=pltpu.CompilerParams(dimension_semantics=("arbitrary",), vmem_limit_bytes=VMEM_LIMIT),
        name="flash_heads",
    )(k16, kb16, qT16, vT16, oT)


def _proj_sample_kernel(x_ref, g_ref, w_ref, wf_ref, bf_ref, lng_ref, lnb_ref, ws_ref, bsT_ref,
                        k_ref, v_ref, lf_ref, vn_ref, q_ref, kaug_ref, v16_ref, ga_ref, gzb_ref):
    rows = x_ref.shape[0]
    t = 8
    ua, vn, za, q, k, v, zb, logf = _project(x_ref[...], g_ref, w_ref, wf_ref, bf_ref, lng_ref, lnb_ref)

    r_i = lax.broadcasted_iota(jnp.int32, (rows, rows), 0)
    c_i = lax.broadcasted_iota(jnp.int32, (rows, rows), 1)
    block = ((r_i // t) == (c_i // t)) & ((c_i % t) <= (r_i % t))
    spread = (lax.broadcasted_iota(jnp.int32, (LANES, rows), 0)
              == lax.broadcasted_iota(jnp.int32, (LANES, rows), 1) % t)
    spread = jnp.where(spread, 1.0, 0.0).astype(bf16)
    mixed = []
    for g in range(A_GROUPS):
        w8 = jnp.dot(ws_ref[g, 0:t, :].astype(bf16), spread, preferred_element_type=f32)
        wt = jnp.broadcast_to(w8[None], (rows // t, t, rows)).reshape(rows, rows)
        wbd = jnp.where(block, wt, 0.0).astype(bf16)
        b8 = jnp.broadcast_to(bsT_ref[0:t, g:g + 1], (t, CHUNK))
        bcol = jnp.broadcast_to(b8[None], (rows // t, t, CHUNK)).reshape(rows, CHUNK)
        mixed.append(jnp.dot(wbd, vn[g].astype(bf16), preferred_element_type=f32) + bcol)
    mix = jnp.concatenate(mixed, axis=1)
    ga_ref[...] = (ua * mix * jax.nn.silu(za)).astype(bf16)
    gzb_ref[...] = jax.nn.silu(zb)
    vn_ref[...] = jnp.concatenate(vn, axis=1)
    k_ref[...] = k
    v_ref[...] = v
    v16_ref[...] = v.astype(bf16)
    lf_ref[...] = logf
    q_ref[...] = q * (QK_SCALE * LOG2E)

    lf3 = logf.reshape(rows // t, t, LANES)
    tok = lax.broadcasted_iota(jnp.int32, lf3.shape, 1)
    c3 = jnp.zeros_like(lf3)
    for s in range(t):
        c3 = c3 + jnp.where(tok >= s, jnp.broadcast_to(lf3[:, s:s + 1, :], lf3.shape), 0.0)
    c_rows = c3.reshape(rows, LANES)
    kaug_ref[:, 0:B_WIDTH] = k.astype(bf16)
    kaug_ref[:, B_WIDTH:B_WIDTH + LANES] = _bias_columns(c_rows * (-LOG2E))


def _proj_sample(x, g, w16, wf16, bfp, lng, lnb, ws, bsT):
    n = x.shape[0]
    rows = PROJ_ROWS
    const = lambda *shape: pl.BlockSpec(shape, lambda i: (0,) * len(shape))
    tile = lambda width: pl.BlockSpec((rows, width), lambda i: (i, 0))
    out_shape = (
        jax.ShapeDtypeStruct((n, B_WIDTH), f32),
        jax.ShapeDtypeStruct((n, B_WIDTH), f32),
        jax.ShapeDtypeStruct((n, LANES), f32),
        jax.ShapeDtypeStruct((n, A_WIDTH), f32),
        jax.ShapeDtypeStruct((n, B_WIDTH), f32),
        jax.ShapeDtypeStruct((n, B_WIDTH + LANES), bf16),
        jax.ShapeDtypeStruct((n, B_WIDTH), bf16),
        jax.ShapeDtypeStruct((n, A_WIDTH), bf16),
        jax.ShapeDtypeStruct((n, B_WIDTH), f32),
    )
    out_specs = (tile(B_WIDTH), tile(B_WIDTH), tile(LANES), tile(A_WIDTH), tile(B_WIDTH),
                 tile(B_WIDTH + LANES), tile(B_WIDTH), tile(A_WIDTH), tile(B_WIDTH))
    return pl.pallas_call(
        _proj_sample_kernel,
        grid=(n // rows,),
        in_specs=[tile(D_MODEL), const(1, D_MODEL), const(D_MODEL, MAIN_WIDTH), const(D_MODEL, LANES),
                  const(1, LANES), const(1, A_WIDTH), const(1, A_WIDTH), const(A_GROUPS, CHUNK, CHUNK),
                  const(CHUNK, A_GROUPS)],
        out_specs=out_specs,
        out_shape=out_shape,
        compiler_params=pltpu.CompilerParams(dimension_semantics=("arbitrary",),
                                             vmem_limit_bytes=VMEM_LIMIT),
        name="proj_sample",
    )(x, g, w16, wf16, bfp, lng, lnb, ws, bsT)


def _past_decay_kernel(pt_ref, *refs):
    f_ref, o_ref = refs
    b = pl.program_id(0)
    n = o_ref.shape[2] // PAGE_SIZE
    lf = jnp.stack([f_ref[pt_ref[b, p]] for p in range(n)], axis=0)
    r_i = lax.broadcasted_iota(jnp.int32, (PAGE_SIZE, 2 * PAGE_SIZE), 0)
    c_i = lax.broadcasted_iota(jnp.int32, (PAGE_SIZE, 2 * PAGE_SIZE), 1)
    tri = jnp.where((r_i >= c_i) | (c_i >= PAGE_SIZE), 1.0, 0.0).astype(bf16)
    sums = sum(jnp.dot(part.reshape(n * B_HEADS, PAGE_SIZE).astype(bf16), tri, preferred_element_type=f32)
               for part in _split3_bf16(lf))
    incl = sums[:, :PAGE_SIZE].reshape(lf.shape)
    later = sums[:, PAGE_SIZE:].reshape(lf.shape)
    later = jnp.concatenate([later[1:], jnp.zeros_like(later[:1])], axis=0)
    sh = 1
    while sh < n:
        later = later + jnp.concatenate([later[sh:], jnp.zeros_like(later[:sh])], axis=0)
        sh *= 2
    d = (incl - lf + later) * LOG2E
    for p in range(n):
        o_ref[0, :, p * PAGE_SIZE:(p + 1) * PAGE_SIZE] = d[p]


def _past_decay(page_table, lf_cache):
    nb, n_pages = page_table.shape

    grid_spec = pltpu.PrefetchScalarGridSpec(
        num_scalar_prefetch=1,
        grid=(nb,),
        in_specs=[pl.BlockSpec(memory_space=pltpu.VMEM)],
        out_specs=pl.BlockSpec((1, B_HEADS, n_pages * PAGE_SIZE), lambda b, pt: (b, 0, 0)),
    )
    return pl.pallas_call(
        _past_decay_kernel,
        grid_spec=grid_spec,
        out_shape=jax.ShapeDtypeStruct((nb, B_HEADS, n_pages * PAGE_SIZE), f32),
        compiler_params=pltpu.CompilerParams(dimension_semantics=("arbitrary",), vmem_limit_bytes=VMEM_LIMIT),
        name="past_decay",
    )(page_table, lf_cache)


class _DecodeOps:
    TOKENS = 8

    def __init__(self, k_page, v_page, n_pages, d_ref, q_ref, kaug_ref, v16_ref, o_ref, qbd_ref, acc_ref, m_ref,
                 l_ref):
        self.k_page, self.v_page, self.n_pages = k_page, v_page, n_pages
        self.d, self.q, self.kaug, self.v16, self.o = d_ref, q_ref, kaug_ref, v16_ref, o_ref
        self.qbd, self.acc, self.m, self.l = qbd_ref, acc_ref, m_ref, l_ref

    def init(self):
        t = self.TOKENS
        rows = B_HEADS * t
        qrep = jnp.concatenate([self.q[0]] * B_HEADS, axis=0)
        r_i = lax.broadcasted_iota(jnp.int32, (rows, B_WIDTH), 0)
        c_i = lax.broadcasted_iota(jnp.int32, (rows, B_WIDTH), 1)
        self.qbd[:, 0:B_WIDTH] = jnp.where((r_i // t) == (c_i // B_HEAD_DIM), qrep, 0.0).astype(bf16)
        r_b = lax.broadcasted_iota(jnp.int32, (rows, LANES), 0) // t
        c_b = lax.broadcasted_iota(jnp.int32, (rows, LANES), 1)
        ones = (c_b == r_b) | (c_b == r_b + B_HEADS) | (c_b == r_b + 2 * B_HEADS)
        self.qbd[:, B_WIDTH:B_WIDTH + LANES] = jnp.where(ones, 1.0, 0.0).astype(bf16)
        self.m[...] = jnp.full_like(self.m, NEG_BIG)
        self.l[...] = jnp.zeros_like(self.l)
        self.acc[...] = jnp.zeros_like(self.acc)

    def update(self, s, vals, contract_lanes):
        m_old = self.m[...]
        m_new = jnp.maximum(m_old, jnp.max(s, axis=1, keepdims=True))
        p = jnp.exp2(s - m_new)
        alpha = jnp.exp2(m_old - m_new)
        self.l[...] = alpha * self.l[...] + jnp.sum(p, axis=1, keepdims=True)
        if contract_lanes:
            pv = lax.dot_general(p.astype(bf16), vals, (((1,), (1,)), ((), ())), preferred_element_type=f32)
        else:
            pv = jnp.dot(p.astype(bf16), vals, preferred_element_type=f32)
        self.acc[...] = alpha * self.acc[...] + pv
        self.m[...] = m_new

    def stages(self):
        t = self.TOKENS
        npg = self.n_pages
        gp = npg // DECODE_GROUPS
        gw = gp * PAGE_SIZE
        scores = []
        pages16 = lambda page, g: jnp.concatenate([page(i).astype(bf16) for i in range(g * gp, (g + 1) * gp)], axis=1)

        def first():
            d = self.d[0]
            dfull = jnp.concatenate([jnp.broadcast_to(d[hh:hh + 1, :], (t, npg * PAGE_SIZE))
                                     for hh in range(B_HEADS)], axis=0)
            for g in range(DECODE_GROUPS):
                scores.append(jnp.dot(self.qbd[:, 0:B_WIDTH], pages16(self.k_page, g), preferred_element_type=f32)
                              + dfull[:, g * gw:(g + 1) * gw])

        def stage(g):
            self.update(scores[g], pages16(self.v_page, g), True)

        return [first] + [functools.partial(stage, g) for g in range(DECODE_GROUPS)]

    def finalize(self):
        t = self.TOKENS
        sn = lax.dot_general(self.qbd[...], self.kaug[0], (((1,), (1,)), ((), ())),
                             preferred_element_type=f32)
        tq = lax.broadcasted_iota(jnp.int32, sn.shape, 0) % t
        tk = lax.broadcasted_iota(jnp.int32, sn.shape, 1)
        self.update(jnp.where(tk <= tq, sn, NEG_BIG), self.v16[0], False)
        res = self.acc[...] / self.l[...]
        r_i = lax.broadcasted_iota(jnp.int32, (t, B_WIDTH), 1) // B_HEAD_DIM
        out = jnp.zeros((t, B_WIDTH), f32)
        for hh in range(B_HEADS):
            out = out + jnp.where(r_i == hh, res[hh * t:(hh + 1) * t, :], 0.0)
        self.o[0] = out


def _interleave(a, b):
    out = []
    for i in range(max(len(a), len(b))):
        out += a[i:i + 1] + b[i:i + 1]
    return out


def _attend_kernel(steps_per_seq, pt_ref, work_ref, *refs):
    npg = PAGES_PER_STEP
    (kc_ref, vc_ref, d_ref, q_ref, kaug_ref, v16_ref, kp_ref, kb_ref, qT_ref, vT_ref, o_dec_ref, o_fl_ref,
     qbd_ref, dacc_ref, dm_ref, dl_ref, qaug_ref, facc_ref, fm_ref, kbuf_ref, vbuf_ref, sem) = refs
    g = pl.program_id(0)
    last = pl.num_programs(0) - 1
    j = g % steps_per_seq
    slot = g % 2
    h, qi, kj = work_ref[1, g], work_ref[2, g], work_ref[3, g]

    def page_copies(step, to_slot):
        seq_i = step // steps_per_seq
        base = (step % steps_per_seq) * npg
        copies = []
        for i in range(npg):
            page = pt_ref[seq_i, base + i]
            copies.append(pltpu.make_async_copy(kc_ref.at[page], kbuf_ref.at[to_slot, i], sem.at[to_slot, 0]))
            copies.append(pltpu.make_async_copy(vc_ref.at[page], vbuf_ref.at[to_slot, i], sem.at[to_slot, 1]))
        return copies

    @pl.when(g == 0)
    def _():
        for c in page_copies(0, 0):
            c.start()

    for c in page_copies(jnp.minimum(g + 1, last), 1 - slot):
        c.start()
    for c in page_copies(g, slot):
        c.wait()

    dec = _DecodeOps(lambda i: kbuf_ref[slot, i], lambda i: vbuf_ref[slot, i], npg, d_ref, q_ref, kaug_ref, v16_ref,
                     o_dec_ref, qbd_ref, dacc_ref, dm_ref, dl_ref)
    fl = _FlashOps(kp_ref, kb_ref, qT_ref, vT_ref, o_fl_ref, qaug_ref, facc_ref, fm_ref, h)

    def run(stages):
        for stage in stages:
            stage()

    pl.when(j == 0)(dec.init)
    pl.when(kj == 0)(lambda: fl.init_tile(qi))

    @pl.when((kj >= 0) & (kj < qi))
    def _():
        run(_interleave(dec.stages(), fl.block_stages(pl.multiple_of(kj * Q_TILE, Q_TILE), False)))

    @pl.when(kj == qi)
    def _():
        run(_interleave(dec.stages(), fl.block_stages(pl.multiple_of(qi * Q_TILE, Q_TILE), True)))
        fl.finalize(qi)

    pl.when(kj < 0)(lambda: run(dec.stages()))
    pl.when(j == steps_per_seq - 1)(dec.finalize)

    @pl.when(g == last)
    def _():
        for c in page_copies(last, 1 - slot):
            c.wait()


def _attend(page_table, kT_cache, vT_cache, decay, q, kaug, v16, k16, kb16, qT16, vT16):
    nb, n_pages = page_table.shape
    bsz, seq, _ = k16.shape
    npg = PAGES_PER_STEP
    steps_per_seq = n_pages // npg
    steps = nb * steps_per_seq
    t = _DecodeOps.TOKENS
    nq = seq // Q_TILE
    items = [(qi, kj) for qi in range(nq) for kj in range(qi + 1)]
    pairs = min(bsz * B_HEADS, steps // len(items))
    work = [(p // B_HEADS, p % B_HEADS, qi, kj) for p in range(pairs) for qi, kj in items]
    work += [((pairs - 1) // B_HEADS, (pairs - 1) % B_HEADS, 0, -1)] * (steps - len(work))
    work = jnp.asarray(work, jnp.int32).T

    per_seq = lambda width: pl.BlockSpec((1, t, width), lambda g, pt, wk: (g // steps_per_seq, 0, 0))
    in_specs = ([pl.BlockSpec(memory_space=pl.ANY), pl.BlockSpec(memory_space=pl.ANY)]
                + [pl.BlockSpec((1, B_HEADS, npg * PAGE_SIZE),
                                lambda g, pt, wk: (g // steps_per_seq, 0, g % steps_per_seq))]
                + [per_seq(B_WIDTH), per_seq(B_WIDTH + LANES), per_seq(B_WIDTH)]
                + [pl.BlockSpec((1, seq, LANES), lambda g, pt, wk: (wk[0, g], 0, wk[1, g] // 2)),
                   pl.BlockSpec((1, seq, LANES), lambda g, pt, wk: (wk[0, g], 0, 0)),
                   pl.BlockSpec((1, LANES, seq), lambda g, pt, wk: (wk[0, g], wk[1, g] // 2, 0)),
                   pl.BlockSpec((1, B_HEAD_DIM, seq), lambda g, pt, wk: (wk[0, g], wk[1, g], 0))])
    grid_spec = pltpu.PrefetchScalarGridSpec(
        num_scalar_prefetch=2,
        grid=(steps,),
        in_specs=in_specs,
        out_specs=(per_seq(B_WIDTH),
                   pl.BlockSpec((1, B_HEAD_DIM, seq), lambda g, pt, wk: (wk[0, g], wk[1, g], 0))),
        scratch_shapes=[pltpu.VMEM((B_HEADS * t, B_WIDTH + LANES), bf16),
                        pltpu.VMEM((B_HEADS * t, B_WIDTH), f32),
                        pltpu.VMEM((B_HEADS * t, 1), f32), pltpu.VMEM((B_HEADS * t, 1), f32),
                        pltpu.VMEM((2 * LANES, Q_TILE), bf16), pltpu.VMEM((B_HEAD_DIM + SUM_ROWS, Q_TILE), f32),
                        pltpu.VMEM((1, Q_TILE), f32),
                        pltpu.VMEM((2, npg, B_WIDTH, PAGE_SIZE), f32), pltpu.VMEM((2, npg, B_WIDTH, PAGE_SIZE), f32),
                        pltpu.SemaphoreType.DMA((2, 2))],
    )
    o_dec, oT = pl.pallas_call(
        functools.partial(_attend_kernel, steps_per_seq),
        grid_spec=grid_spec,
        out_shape=(jax.ShapeDtypeStruct((nb, t, B_WIDTH), f32), jax.ShapeDtypeStruct((bsz, B_WIDTH, seq), f32)),
        compiler_params=pltpu.CompilerParams(dimension_semantics=("arbitrary",), vmem_limit_bytes=VMEM_LIMIT),
        name="attend",
    )(page_table, work, kT_cache, vT_cache, decay,
      q.reshape(nb, t, B_WIDTH), kaug.reshape(nb, t, B_WIDTH + LANES), v16.reshape(nb, t, B_WIDTH),
      k16, kb16, qT16, vT16)
    return o_dec, oT, pairs


def _finish_kernel(transposed, x_ref, ga_ref, o_ref, gzb_ref, p_ref, wo_ref, gpost_ref, wpg_ref, bpg_ref,
                   wpe_ref, y_ref):
    batched = x_ref.ndim == 3
    ld = (lambda r: r[0]) if batched else (lambda r: r[...])
    x = ld(x_ref)
    o = ld(o_ref)
    if transposed:
        o = o.T
    mix_b = (o * ld(gzb_ref)).astype(bf16)
    y = (jnp.dot(ld(ga_ref), wo_ref[0:A_WIDTH, :], preferred_element_type=f32)
         + jnp.dot(mix_b, wo_ref[A_WIDTH:, :], preferred_element_type=f32))
    ms = jnp.mean(y * y, axis=-1, keepdims=True)
    x = x + y * lax.rsqrt(ms + RMS_EPS) * gpost_ref[...]
    gate = jax.nn.sigmoid(jnp.dot(x.astype(bf16), wpg_ref[...], preferred_element_type=f32) + bpg_ref[...])
    res = x + gate * jnp.dot(ld(p_ref).astype(bf16), wpe_ref[...], preferred_element_type=f32)
    if batched:
        y_ref[0] = res
    else:
        y_ref[...] = res


def _finish_prompt(x, ga, oT, gzb, p, wo16, gpost, wpg16, bpg, wpe16):
    bsz, seq, _ = x.shape
    rows = PROJ_ROWS
    const = lambda *shape: pl.BlockSpec(shape, lambda b, i: (0,) * len(shape))
    tile_rows = lambda width: pl.BlockSpec((1, rows, width), lambda b, i: (b, i, 0))
    return pl.pallas_call(
        functools.partial(_finish_kernel, True),
        grid=(bsz, seq // rows),
        in_specs=[tile_rows(D_MODEL), tile_rows(A_WIDTH),
                  pl.BlockSpec((1, B_WIDTH, rows), lambda b, i: (b, 0, i)),
                  tile_rows(B_WIDTH), tile_rows(PLE_DIM), const(D_MODEL, D_MODEL), const(1, D_MODEL),
                  const(D_MODEL, D_MODEL), const(1, D_MODEL), const(PLE_DIM, D_MODEL)],
        out_specs=tile_rows(D_MODEL),
        out_shape=jax.ShapeDtypeStruct(x.shape, f32),
        compiler_params=pltpu.CompilerParams(dimension_semantics=("arbitrary", "arbitrary"),
                                             vmem_limit_bytes=VMEM_LIMIT),
        name="finish_prompt",
    )(x, ga, oT, gzb, p, wo16, gpost, wpg16, bpg, wpe16)


def _finish_sample(x, ga, o, gzb, p, wo16, gpost, wpg16, bpg, wpe16):
    n = x.shape[0]
    rows = PROJ_ROWS
    const = lambda *shape: pl.BlockSpec(shape, lambda i: (0,) * len(shape))
    tile = lambda width: pl.BlockSpec((rows, width), lambda i: (i, 0))
    return pl.pallas_call(
        functools.partial(_finish_kernel, False),
        grid=(n // rows,),
        in_specs=[tile(D_MODEL), tile(A_WIDTH), tile(B_WIDTH), tile(B_WIDTH), tile(PLE_DIM),
                  const(D_MODEL, D_MODEL), const(1, D_MODEL), const(D_MODEL, D_MODEL), const(1, D_MODEL),
                  const(PLE_DIM, D_MODEL)],
        out_specs=tile(D_MODEL),
        out_shape=jax.ShapeDtypeStruct(x.shape, f32),
        compiler_params=pltpu.CompilerParams(dimension_semantics=("arbitrary",),
                                             vmem_limit_bytes=VMEM_LIMIT),
        name="finish_sample",
    )(x, ga, o, gzb, p, wo16, gpost, wpg16, bpg, wpe16)


def kernel(x_prompt, x_sample, cache_k, cache_v, cache_logf, page_table, p_prompt, p_sample, ln_pre_g, w_in, b_f,
           ln_v_g, ln_v_b, w_s, b_s, w_out, ln_post_g, w_pe, w_pg, b_pg):
    assert w_in.shape[0] == 1, "single-layer step"
    bsz, seq, _ = x_prompt.shape
    nb, t, _ = x_sample.shape
    n_pool = cache_k.shape[1]

    w16 = w_in[0, :, :MAIN_WIDTH].astype(bf16)
    wf16 = jnp.pad(w_in[0, :, MAIN_WIDTH:], ((0, 0), (0, LANES - B_HEADS))).astype(bf16)
    bfp = jnp.pad(b_f[0], (0, LANES - B_HEADS)).reshape(1, LANES)
    g_pre = ln_pre_g[0].reshape(1, D_MODEL)
    lng = ln_v_g[0].reshape(1, A_WIDTH)
    lnb = ln_v_b[0].reshape(1, A_WIDTH)
    bsT = b_s[0].T
    wo16 = w_out[0].astype(bf16)
    wpg16 = w_pg[0].astype(bf16)
    wpe16 = w_pe[0].astype(bf16)
    gpost = ln_post_g[0].reshape(1, D_MODEL)
    bpg = b_pg[0].reshape(1, D_MODEL)

    kT, vT, lfT, qT16, k16, kb16, vT16, ga, gzb = _proj_prompt(x_prompt, g_pre, w16, wf16, bfp, lng, lnb, w_s[0], bsT)
    new_k_prompt = kT.reshape(bsz, B_HEADS, B_HEAD_DIM, seq).transpose(0, 3, 1, 2)[None]
    new_v_prompt = vT.reshape(bsz, B_HEADS, B_HEAD_DIM, seq).transpose(0, 3, 1, 2)[None]
    new_logf_prompt = lfT.transpose(0, 2, 1)[None]

    kT_cache = cache_k[0].transpose(0, 2, 3, 1).reshape(n_pool, B_WIDTH, PAGE_SIZE)
    vT_cache = cache_v[0].transpose(0, 2, 3, 1).reshape(n_pool, B_WIDTH, PAGE_SIZE)
    lf_cache = cache_logf[0].transpose(0, 2, 1)
    xs = x_sample.reshape(nb * t, D_MODEL)
    ks, vs, lfs, vns, qs, kaug, vs16, gas, gzbs = _proj_sample(xs, g_pre, w16, wf16, bfp, lng, lnb, w_s[0], bsT)
    decay = _past_decay(page_table, lf_cache)
    o_s, oT, pairs = _attend(page_table, kT_cache, vT_cache, decay, qs, kaug, vs16, k16, kb16, qT16, vT16)
    for b0 in range(pairs // B_HEADS, bsz):
        h0 = pairs % B_HEADS if b0 == pairs // B_HEADS else 0
        oT = _flash_heads(k16, kb16, qT16, vT16, oT, b0, h0, B_HEADS - h0)
    y_prompt = _finish_prompt(x_prompt, ga, oT, gzb, p_prompt[0], wo16, gpost, wpg16, bpg, wpe16)
    y_sample = _finish_sample(xs, gas, o_s.reshape(nb * t, B_WIDTH), gzbs, p_sample[0].reshape(nb * t, PLE_DIM),
                              wo16, gpost, wpg16, bpg, wpe16).reshape(nb, t, D_MODEL)

    return (y_prompt, y_sample, new_k_prompt, new_v_prompt, new_logf_prompt,
            ks.reshape(1, nb, t, B_HEADS, B_HEAD_DIM), vs.reshape(1, nb, t, B_HEADS, B_HEAD_DIM),
            lfs[:, :B_HEADS].reshape(1, nb, t, B_HEADS), vns.reshape(1, nb, t, A_GROUPS, CHUNK))
```

```python
import functools
import math

import jax
import jax.numpy as jnp
from jax import lax
from jax.experimental import pallas as pl
from jax.experimental.pallas import tpu as pltpu

D_MODEL = 1024
A_WIDTH = 512
B_WIDTH = 512
CHUNK = 128
A_GROUPS = 4
B_HEADS = 8
B_HEAD_DIM = 64
PLE_DIM = 256
PAGE_SIZE = 128
RMS_EPS = 1e-6
LN_EPS = 1e-5
MAIN_WIDTH = 3 * A_WIDTH + 4 * B_WIDTH
LANES = 128
LOG2E = math.log2(math.e)
QK_SCALE = B_HEAD_DIM ** -0.5
NEG_BIG = -1e30

PROJ_ROWS = 512
Q_TILE = 1024
K_SUB = 256
SUM_ROWS = 16
LOOKAHEAD = 3
PAGES_PER_STEP = 16
DECODE_GROUPS = 4
VMEM_LIMIT = 56 * 1024 * 1024

f32 = jnp.float32
bf16 = jnp.bfloat16


def _log_sigmoid(x):
    return jnp.minimum(x, 0.0) - jnp.log1p(jnp.exp(-jnp.abs(x)))


def _split3_bf16(x):
    hi = x.astype(bf16).astype(f32)
    r = x - hi
    mid = r.astype(bf16).astype(f32)
    lo = (r - mid).astype(bf16).astype(f32)
    return hi, mid, lo


def _bias_columns(nb):
    hi, mid, lo = _split3_bf16(nb)
    packed = hi + pltpu.roll(mid, B_HEADS, axis=1) + pltpu.roll(lo, 2 * B_HEADS, axis=1)
    return packed.astype(bf16)


def _project(x, g_ref, w_ref, wf_ref, bf_ref, lng_ref, lnb_ref):
    ms = jnp.mean(x * x, axis=-1, keepdims=True)
    h = (x * lax.rsqrt(ms + RMS_EPS) * g_ref[...]).astype(bf16)

    def sec(i):
        return jnp.dot(h, w_ref[:, i * 512:(i + 1) * 512], preferred_element_type=f32)

    ua = jax.nn.gelu(sec(0))
    va = jax.nn.gelu(sec(1))
    za = sec(2)
    q = sec(3)
    k = sec(4)
    v = sec(5)
    zb = sec(6)
    f = jnp.dot(h, wf_ref[...], preferred_element_type=f32) + bf_ref[...]
    col = lax.broadcasted_iota(jnp.int32, f.shape, 1)
    logf = jnp.where(col < B_HEADS, _log_sigmoid(f), 0.0)
    vn = []
    for g in range(A_GROUPS):
        vg = va[:, g * CHUNK:(g + 1) * CHUNK]
        mu = jnp.mean(vg, axis=-1, keepdims=True)
        var = jnp.mean(jnp.square(vg - mu), axis=-1, keepdims=True)
        vn.append((vg - mu) * lax.rsqrt(var + LN_EPS) * lng_ref[:, g * CHUNK:(g + 1) * CHUNK]
                  + lnb_ref[:, g * CHUNK:(g + 1) * CHUNK])
    return ua, vn, za, q, k, v, zb, logf


def _lane_cumsum(x):
    n = x.shape[-1]
    idx = lax.broadcasted_iota(jnp.int32, x.shape, x.ndim - 1)
    sh = 1
    while sh < n:
        x = x + jnp.where(idx >= sh, pltpu.roll(x, sh, axis=x.ndim - 1), 0.0)
        sh *= 2
    return x


def _lane_suffix_sum(x):
    n = x.shape[-1]
    idx = lax.broadcasted_iota(jnp.int32, x.shape, x.ndim - 1)
    sh = 1
    while sh < n:
        x = x + jnp.where(idx < n - sh, pltpu.roll(x, n - sh, axis=x.ndim - 1), 0.0)
        sh *= 2
    return x


def _proj_prompt_kernel(x_ref, g_ref, w_ref, wf_ref, bf_ref, lng_ref, lnb_ref, ws_ref, bsT_ref,
                        kT_ref, vT_ref, lfT_ref, qT16_ref, k16_ref, kb16_ref, vT16_ref, ga_ref, gzb_ref,
                        carry_ref):
    rows = x_ref.shape[1]

    @pl.when(pl.program_id(1) == 0)
    def _():
        carry_ref[...] = jnp.zeros_like(carry_ref)

    ua, vn, za, q, k, v, zb, logf = _project(x_ref[0], g_ref, w_ref, wf_ref, bf_ref, lng_ref, lnb_ref)

    tri = (lax.broadcasted_iota(jnp.int32, (CHUNK, CHUNK), 0)
           >= lax.broadcasted_iota(jnp.int32, (CHUNK, CHUNK), 1))
    mixed = []
    for g in range(A_GROUPS):
        wm = jnp.where(tri, ws_ref[g], 0.0).astype(bf16)
        bcol = jnp.broadcast_to(bsT_ref[:, g:g + 1], (CHUNK, CHUNK))
        vg = vn[g].astype(bf16)
        parts = [jnp.dot(wm, vg[c * CHUNK:(c + 1) * CHUNK], preferred_element_type=f32) + bcol
                 for c in range(rows // CHUNK)]
        mixed.append(jnp.concatenate(parts, axis=0))
    mix = jnp.concatenate(mixed, axis=1)
    ga_ref[0] = (ua * mix * jax.nn.silu(za)).astype(bf16)
    gzb_ref[0] = jax.nn.silu(zb)

    kT = k.T
    vT = v.T
    kT_ref[0] = kT
    vT_ref[0] = vT
    vT16_ref[0] = vT.astype(bf16)
    k16_ref[0] = k.astype(bf16)
    qT16_ref[0] = (q * (QK_SCALE * LOG2E)).T.astype(bf16)

    lfT = logf.T[0:B_HEADS, :]
    lfT_ref[0] = lfT
    cT = _lane_cumsum(lfT) + carry_ref[:, 0:1]
    carry_ref[...] = jnp.broadcast_to(cT[:, rows - 1:rows], carry_ref.shape)
    c_rows = jnp.concatenate([cT, jnp.zeros((LANES - B_HEADS, rows), f32)], axis=0).T
    kb16_ref[0] = _bias_columns(c_rows * (-LOG2E))


def _proj_prompt(x, g, w16, wf16, bfp, lng, lnb, ws, bsT):
    bsz, seq, _ = x.shape
    rows = PROJ_ROWS
    const = lambda *shape: pl.BlockSpec(shape, lambda b, i: (0,) * len(shape))
    tile_rows = lambda width: pl.BlockSpec((1, rows, width), lambda b, i: (b, i, 0))
    tile_cols = lambda height: pl.BlockSpec((1, height, rows), lambda b, i: (b, 0, i))
    out_shape = (
        jax.ShapeDtypeStruct((bsz, B_WIDTH, seq), f32),
        jax.ShapeDtypeStruct((bsz, B_WIDTH, seq), f32),
        jax.ShapeDtypeStruct((bsz, B_HEADS, seq), f32),
        jax.ShapeDtypeStruct((bsz, B_WIDTH, seq), bf16),
        jax.ShapeDtypeStruct((bsz, seq, B_WIDTH), bf16),
        jax.ShapeDtypeStruct((bsz, seq, LANES), bf16),
        jax.ShapeDtypeStruct((bsz, B_WIDTH, seq), bf16),
        jax.ShapeDtypeStruct((bsz, seq, A_WIDTH), bf16),
        jax.ShapeDtypeStruct((bsz, seq, B_WIDTH), f32),
    )
    out_specs = (tile_cols(B_WIDTH), tile_cols(B_WIDTH), tile_cols(B_HEADS), tile_cols(B_WIDTH),
                 tile_rows(B_WIDTH), tile_rows(LANES), tile_cols(B_WIDTH), tile_rows(A_WIDTH), tile_rows(B_WIDTH))
    return pl.pallas_call(
        _proj_prompt_kernel,
        grid=(bsz, seq // rows),
        in_specs=[tile_rows(D_MODEL), const(1, D_MODEL), const(D_MODEL, MAIN_WIDTH), const(D_MODEL, LANES),
                  const(1, LANES), const(1, A_WIDTH), const(1, A_WIDTH), const(A_GROUPS, CHUNK, CHUNK),
                  const(CHUNK, A_GROUPS)],
        out_specs=out_specs,
        out_shape=out_shape,
        scratch_shapes=[pltpu.VMEM((B_HEADS, LANES), f32)],
        compiler_params=pltpu.CompilerParams(dimension_semantics=("arbitrary", "arbitrary"),
                                             vmem_limit_bytes=VMEM_LIMIT),
        name="proj_prompt",
    )(x, g, w16, wf16, bfp, lng, lnb, ws, bsT)


class _FlashOps:
    def __init__(self, kp_ref, kb_ref, qT_ref, vT_ref, o_ref, qaug_ref, acc_ref, m_ref, h):
        self.kp, self.kb, self.qT, self.vT, self.o = kp_ref, kb_ref, qT_ref, vT_ref, o_ref
        self.qaug, self.acc, self.m = qaug_ref, acc_ref, m_ref
        t, ks = Q_TILE, K_SUB
        self.par = h % 2
        self.row = lax.broadcasted_iota(jnp.int32, (LANES, t), 0)
        self.ones_rows = jnp.where((self.row == h) | (self.row == h + B_HEADS) | (self.row == h + 2 * B_HEADS),
                                   1.0, 0.0).astype(bf16)
        self.causal = lax.broadcasted_iota(jnp.int32, (ks, t), 0) <= lax.broadcasted_iota(jnp.int32, (ks, t), 1)
        self.sum_rows = jnp.ones((SUM_ROWS, ks), bf16)

    def scores(self, k0, lo):
        kt = jnp.concatenate([self.kp[0, pl.ds(k0, K_SUB), :], self.kb[0, pl.ds(k0, K_SUB), :]], axis=1)
        return jnp.dot(kt, self.qaug[:, lo:], preferred_element_type=f32)

    def consume(self, s, k0, lo, masked):
        if masked:
            s = jnp.where(self.causal[:, :Q_TILE - lo], s, NEG_BIG)
        m_old = self.m[:, lo:]
        m_new = jnp.maximum(m_old, jnp.max(s, axis=0, keepdims=True))
        p = jnp.exp2(s - m_new).astype(bf16)
        alpha = jnp.exp2(m_old - m_new)
        v_aug = jnp.concatenate([self.vT[0, :, pl.ds(k0, K_SUB)], self.sum_rows], axis=0)
        self.acc[:, lo:] = alpha * self.acc[:, lo:] + jnp.dot(v_aug, p, preferred_element_type=f32)
        self.m[:, lo:] = m_new

    def init_tile(self, qi):
        q0 = pl.multiple_of(qi * Q_TILE, Q_TILE)
        qp = self.qT[0, :, pl.ds(q0, Q_TILE)].astype(f32)
        self.qaug[0:LANES, :] = jnp.where((self.row // B_HEAD_DIM) == self.par, qp, 0.0).astype(bf16)
        self.qaug[LANES:2 * LANES, :] = self.ones_rows
        self.m[...] = jnp.full_like(self.m, NEG_BIG)
        self.acc[...] = jnp.zeros_like(self.acc)

    def block_stages(self, k_base, diagonal):
        nsub = Q_TILE // K_SUB
        ahead = []
        lo = lambda r: r * K_SUB if diagonal else 0
        k_of = lambda r: pl.multiple_of(k_base + r * K_SUB, K_SUB)

        def first():
            for r in range(min(LOOKAHEAD, nsub)):
                ahead.append(self.scores(k_of(r), lo(r)))

        def stage(r):
            if r + LOOKAHEAD < nsub:
                ahead.append(self.scores(k_of(r + LOOKAHEAD), lo(r + LOOKAHEAD)))
            self.consume(ahead.pop(0), k_of(r), lo(r), diagonal)

        return [first] + [functools.partial(stage, r) for r in range(nsub)]

    def finalize(self, qi):
        q0 = pl.multiple_of(qi * Q_TILE, Q_TILE)
        self.o[0, :, pl.ds(q0, Q_TILE)] = self.acc[0:B_HEAD_DIM, :] / self.acc[B_HEAD_DIM:B_HEAD_DIM + 1, :]


def _flash_kernel(h0, kp_ref, kb_ref, qT_ref, vT_ref, o_prev_ref, o_ref, qaug_ref, acc_ref, m_ref):
    del o_prev_ref
    ops = _FlashOps(kp_ref, kb_ref, qT_ref, vT_ref, o_ref, qaug_ref, acc_ref, m_ref, h0 + pl.program_id(0))

    def q_body(qi, carry):
        ops.init_tile(qi)

        def kv_body(kj, c):
            for stage in ops.block_stages(pl.multiple_of(kj * Q_TILE, Q_TILE), False):
                stage()
            return c

        lax.fori_loop(0, qi, kv_body, 0)
        for stage in ops.block_stages(pl.multiple_of(qi * Q_TILE, Q_TILE), True):
            stage()
        ops.finalize(qi)
        return carry

    lax.fori_loop(0, kp_ref.shape[1] // Q_TILE, q_body, 0)


def _flash_heads(k16, kb16, qT16, vT16, oT, b0, h0, nh):
    seq = k16.shape[1]
    t = Q_TILE
    return pl.pallas_call(
        functools.partial(_flash_kernel, h0),
        grid=(nh,),
        in_specs=[pl.BlockSpec((1, seq, LANES), lambda h: (b0, 0, (h0 + h) // 2)),
                  pl.BlockSpec((1, seq, LANES), lambda h: (b0, 0, 0)),
                  pl.BlockSpec((1, LANES, seq), lambda h: (b0, (h0 + h) // 2, 0)),
                  pl.BlockSpec((1, B_HEAD_DIM, seq), lambda h: (b0, h0 + h, 0)),
                  pl.BlockSpec(memory_space=pl.ANY)],
        out_specs=pl.BlockSpec((1, B_HEAD_DIM, seq), lambda h: (b0, h0 + h, 0)),
        out_shape=jax.ShapeDtypeStruct(oT.shape, f32),
        input_output_aliases={4: 0},
        scratch_shapes=[pltpu.VMEM((2 * LANES, t), bf16), pltpu.VMEM((B_HEAD_DIM + SUM_ROWS, t), f32),
                        pltpu.VMEM((1, t), f32)],
        compiler_params=pltpu.CompilerParams(dimension_semantics=("arbitrary",), vmem_limit_bytes=VMEM_LIMIT),
        name="flash_heads",
    )(k16, kb16, qT16, vT16, oT)


def _proj_sample_kernel(x_ref, g_ref, w_ref, wf_ref, bf_ref, lng_ref, lnb_ref, ws_ref, bsT_ref,
                        k_ref, v_ref, lf_ref, vn_ref, q_ref, kaug_ref, v16_ref, ga_ref, gzb_ref):
    rows = x_ref.shape[0]
    t = 8
    ua, vn, za, q, k, v, zb, logf = _project(x_ref[...], g_ref, w_ref, wf_ref, bf_ref, lng_ref, lnb_ref)

    r_i = lax.broadcasted_iota(jnp.int32, (rows, rows), 0)
    c_i = lax.broadcasted_iota(jnp.int32, (rows, rows), 1)
    block = ((r_i // t) == (c_i // t)) & ((c_i % t) <= (r_i % t))
    spread = (lax.broadcasted_iota(jnp.int32, (LANES, rows), 0)
              == lax.broadcasted_iota(jnp.int32, (LANES, rows), 1) % t)
    spread = jnp.where(spread, 1.0, 0.0).astype(bf16)
    mixed = []
    for g in range(A_GROUPS):
        w8 = jnp.dot(ws_ref[g, 0:t, :].astype(bf16), spread, preferred_element_type=f32)
        wt = jnp.broadcast_to(w8[None], (rows // t, t, rows)).reshape(rows, rows)
        wbd = jnp.where(block, wt, 0.0).astype(bf16)
        b8 = jnp.broadcast_to(bsT_ref[0:t, g:g + 1], (t, CHUNK))
        bcol = jnp.broadcast_to(b8[None], (rows // t, t, CHUNK)).reshape(rows, CHUNK)
        mixed.append(jnp.dot(wbd, vn[g].astype(bf16), preferred_element_type=f32) + bcol)
    mix = jnp.concatenate(mixed, axis=1)
    ga_ref[...] = (ua * mix * jax.nn.silu(za)).astype(bf16)
    gzb_ref[...] = jax.nn.silu(zb)
    vn_ref[...] = jnp.concatenate(vn, axis=1)
    k_ref[...] = k
    v_ref[...] = v
    v16_ref[...] = v.astype(bf16)
    lf_ref[...] = logf
    q_ref[...] = q * (QK_SCALE * LOG2E)

    lf3 = logf.reshape(rows // t, t, LANES)
    tok = lax.broadcasted_iota(jnp.int32, lf3.shape, 1)
    c3 = jnp.zeros_like(lf3)
    for s in range(t):
        c3 = c3 + jnp.where(tok >= s, jnp.broadcast_to(lf3[:, s:s + 1, :], lf3.shape), 0.0)
    c_rows = c3.reshape(rows, LANES)
    kaug_ref[:, 0:B_WIDTH] = k.astype(bf16)
    kaug_ref[:, B_WIDTH:B_WIDTH + LANES] = _bias_columns(c_rows * (-LOG2E))


def _proj_sample(x, g, w16, wf16, bfp, lng, lnb, ws, bsT):
    n = x.shape[0]
    rows = PROJ_ROWS
    const = lambda *shape: pl.BlockSpec(shape, lambda i: (0,) * len(shape))
    tile = lambda width: pl.BlockSpec((rows, width), lambda i: (i, 0))
    out_shape = (
        jax.ShapeDtypeStruct((n, B_WIDTH), f32),
        jax.ShapeDtypeStruct((n, B_WIDTH), f32),
        jax.ShapeDtypeStruct((n, LANES), f32),
        jax.ShapeDtypeStruct((n, A_WIDTH), f32),
        jax.ShapeDtypeStruct((n, B_WIDTH), f32),
        jax.ShapeDtypeStruct((n, B_WIDTH + LANES), bf16),
        jax.ShapeDtypeStruct((n, B_WIDTH), bf16),
        jax.ShapeDtypeStruct((n, A_WIDTH), bf16),
        jax.ShapeDtypeStruct((n, B_WIDTH), f32),
    )
    out_specs = (tile(B_WIDTH), tile(B_WIDTH), tile(LANES), tile(A_WIDTH), tile(B_WIDTH),
                 tile(B_WIDTH + LANES), tile(B_WIDTH), tile(A_WIDTH), tile(B_WIDTH))
    return pl.pallas_call(
        _proj_sample_kernel,
        grid=(n // rows,),
        in_specs=[tile(D_MODEL), const(1, D_MODEL), const(D_MODEL, MAIN_WIDTH), const(D_MODEL, LANES),
                  const(1, LANES), const(1, A_WIDTH), const(1, A_WIDTH), const(A_GROUPS, CHUNK, CHUNK),
                  const(CHUNK, A_GROUPS)],
        out_specs=out_specs,
        out_shape=out_shape,
        compiler_params=pltpu.CompilerParams(dimension_semantics=("arbitrary",),
                                             vmem_limit_bytes=VMEM_LIMIT),
        name="proj_sample",
    )(x, g, w16, wf16, bfp, lng, lnb, ws, bsT)


def _past_decay_kernel(pt_ref, *refs):
    f_ref, o_ref = refs
    b = pl.program_id(0)
    n = o_ref.shape[2] // PAGE_SIZE
    lf = jnp.stack([f_ref[pt_ref[b, p]] for p in range(n)], axis=0)
    r_i = lax.broadcasted_iota(jnp.int32, (PAGE_SIZE, 2 * PAGE_SIZE), 0)
    c_i = lax.broadcasted_iota(jnp.int32, (PAGE_SIZE, 2 * PAGE_SIZE), 1)
    tri = jnp.where((r_i >= c_i) | (c_i >= PAGE_SIZE), 1.0, 0.0).astype(bf16)
    sums = sum(jnp.dot(part.reshape(n * B_HEADS, PAGE_SIZE).astype(bf16), tri, preferred_element_type=f32)
               for part in _split3_bf16(lf))
    incl = sums[:, :PAGE_SIZE].reshape(lf.shape)
    later = sums[:, PAGE_SIZE:].reshape(lf.shape)
    later = jnp.concatenate([later[1:], jnp.zeros_like(later[:1])], axis=0)
    sh = 1
    while sh < n:
        later = later + jnp.concatenate([later[sh:], jnp.zeros_like(later[:sh])], axis=0)
        sh *= 2
    d = (incl - lf + later) * LOG2E
    for p in range(n):
        o_ref[0, :, p * PAGE_SIZE:(p + 1) * PAGE_SIZE] = d[p]


def _past_decay(page_table, lf_cache):
    nb, n_pages = page_table.shape

    grid_spec = pltpu.PrefetchScalarGridSpec(
        num_scalar_prefetch=1,
        grid=(nb,),
        in_specs=[pl.BlockSpec(memory_space=pltpu.VMEM)],
        out_specs=pl.BlockSpec((1, B_HEADS, n_pages * PAGE_SIZE), lambda b, pt: (b, 0, 0)),
    )
    return pl.pallas_call(
        _past_decay_kernel,
        grid_spec=grid_spec,
        out_shape=jax.ShapeDtypeStruct((nb, B_HEADS, n_pages * PAGE_SIZE), f32),
        compiler_params=pltpu.CompilerParams(dimension_semantics=("arbitrary",), vmem_limit_bytes=VMEM_LIMIT),
        name="past_decay",
    )(page_table, lf_cache)


class _DecodeOps:
    TOKENS = 8

    def __init__(self, k_page, v_page, n_pages, d_ref, q_ref, kaug_ref, v16_ref, o_ref, qbd_ref, acc_ref, m_ref,
                 l_ref):
        self.k_page, self.v_page, self.n_pages = k_page, v_page, n_pages
        self.d, self.q, self.kaug, self.v16, self.o = d_ref, q_ref, kaug_ref, v16_ref, o_ref
        self.qbd, self.acc, self.m, self.l = qbd_ref, acc_ref, m_ref, l_ref

    def init(self):
        t = self.TOKENS
        rows = B_HEADS * t
        qrep = jnp.concatenate([self.q[0]] * B_HEADS, axis=0)
        r_i = lax.broadcasted_iota(jnp.int32, (rows, B_WIDTH), 0)
        c_i = lax.broadcasted_iota(jnp.int32, (rows, B_WIDTH), 1)
        self.qbd[:, 0:B_WIDTH] = jnp.where((r_i // t) == (c_i // B_HEAD_DIM), qrep, 0.0).astype(bf16)
        r_b = lax.broadcasted_iota(jnp.int32, (rows, LANES), 0) // t
        c_b = lax.broadcasted_iota(jnp.int32, (rows, LANES), 1)
        ones = (c_b == r_b) | (c_b == r_b + B_HEADS) | (c_b == r_b + 2 * B_HEADS)
        self.qbd[:, B_WIDTH:B_WIDTH + LANES] = jnp.where(ones, 1.0, 0.0).astype(bf16)
        self.m[...] = jnp.full_like(self.m, NEG_BIG)
        self.l[...] = jnp.zeros_like(self.l)
        self.acc[...] = jnp.zeros_like(self.acc)

    def update(self, s, vals, contract_lanes):
        m_old = self.m[...]
        m_new = jnp.maximum(m_old, jnp.max(s, axis=1, keepdims=True))
        p = jnp.exp2(s - m_new)
        alpha = jnp.exp2(m_old - m_new)
        self.l[...] = alpha * self.l[...] + jnp.sum(p, axis=1, keepdims=True)
        if contract_lanes:
            pv = lax.dot_general(p.astype(bf16), vals, (((1,), (1,)), ((), ())), preferred_element_type=f32)
        else:
            pv = jnp.dot(p.astype(bf16), vals, preferred_element_type=f32)
        self.acc[...] = alpha * self.acc[...] + pv
        self.m[...] = m_new

    def stages(self):
        t = self.TOKENS
        npg = self.n_pages
        gp = npg // DECODE_GROUPS
        gw = gp * PAGE_SIZE
        scores = []
        pages16 = lambda page, g: jnp.concatenate([page(i).astype(bf16) for i in range(g * gp, (g + 1) * gp)], axis=1)

        def first():
            d = self.d[0]
            dfull = jnp.concatenate([jnp.broadcast_to(d[hh:hh + 1, :], (t, npg * PAGE_SIZE))
                                     for hh in range(B_HEADS)], axis=0)
            for g in range(DECODE_GROUPS):
                scores.append(jnp.dot(self.qbd[:, 0:B_WIDTH], pages16(self.k_page, g), preferred_element_type=f32)
                              + dfull[:, g * gw:(g + 1) * gw])

        def stage(g):
            self.update(scores[g], pages16(self.v_page, g), True)

        return [first] + [functools.partial(stage, g) for g in range(DECODE_GROUPS)]

    def finalize(self):
        t = self.TOKENS
        sn = lax.dot_general(self.qbd[...], self.kaug[0], (((1,), (1,)), ((), ())),
                             preferred_element_type=f32)
        tq = lax.broadcasted_iota(jnp.int32, sn.shape, 0) % t
        tk = lax.broadcasted_iota(jnp.int32, sn.shape, 1)
        self.update(jnp.where(tk <= tq, sn, NEG_BIG), self.v16[0], False)
        res = self.acc[...] / self.l[...]
        r_i = lax.broadcasted_iota(jnp.int32, (t, B_WIDTH), 1) // B_HEAD_DIM
        out = jnp.zeros((t, B_WIDTH), f32)
        for hh in range(B_HEADS):
            out = out + jnp.where(r_i == hh, res[hh * t:(hh + 1) * t, :], 0.0)
        self.o[0] = out


def _interleave(a, b):
    out = []
    for i in range(max(len(a), len(b))):
        out += a[i:i + 1] + b[i:i + 1]
    return out


def _attend_kernel(steps_per_seq, pt_ref, work_ref, *refs):
    npg = PAGES_PER_STEP
    (kc_ref, vc_ref, d_ref, q_ref, kaug_ref, v16_ref, kp_ref, kb_ref, qT_ref, vT_ref, o_dec_ref, o_fl_ref,
     qbd_ref, dacc_ref, dm_ref, dl_ref, qaug_ref, facc_ref, fm_ref, kbuf_ref, vbuf_ref, sem) = refs
    g = pl.program_id(0)
    last = pl.num_programs(0) - 1
    j = g % steps_per_seq
    slot = g % 2
    h, qi, kj = work_ref[1, g], work_ref[2, g], work_ref[3, g]

    def page_copies(step, to_slot):
        seq_i = step // steps_per_seq
        base = (step % steps_per_seq) * npg
        copies = []
        for i in range(npg):
            page = pt_ref[seq_i, base + i]
            copies.append(pltpu.make_async_copy(kc_ref.at[page], kbuf_ref.at[to_slot, i], sem.at[to_slot, 0]))
            copies.append(pltpu.make_async_copy(vc_ref.at[page], vbuf_ref.at[to_slot, i], sem.at[to_slot, 1]))
        return copies

    @pl.when(g == 0)
    def _():
        for c in page_copies(0, 0):
            c.start()

    for c in page_copies(jnp.minimum(g + 1, last), 1 - slot):
        c.start()
    for c in page_copies(g, slot):
        c.wait()

    dec = _DecodeOps(lambda i: kbuf_ref[slot, i], lambda i: vbuf_ref[slot, i], npg, d_ref, q_ref, kaug_ref, v16_ref,
                     o_dec_ref, qbd_ref, dacc_ref, dm_ref, dl_ref)
    fl = _FlashOps(kp_ref, kb_ref, qT_ref, vT_ref, o_fl_ref, qaug_ref, facc_ref, fm_ref, h)

    def run(stages):
        for stage in stages:
            stage()

    pl.when(j == 0)(dec.init)
    pl.when(kj == 0)(lambda: fl.init_tile(qi))

    def step_body(seq_ends, prompt):
        dstages = dec.stages() + ([dec.finalize] if seq_ends else [])
        if prompt == "full":
            run(_interleave(dstages, fl.block_stages(pl.multiple_of(kj * Q_TILE, Q_TILE), False)))
        elif prompt == "diagonal":
            run(_interleave(dstages, fl.block_stages(pl.multiple_of(qi * Q_TILE, Q_TILE), True)))
            fl.finalize(qi)
        else:
            run(dstages)

    for seq_ends in (False, True):
        ends = (j == steps_per_seq - 1) if seq_ends else (j != steps_per_seq - 1)
        pl.when(ends & (kj >= 0) & (kj < qi))(functools.partial(step_body, seq_ends, "full"))
        pl.when(ends & (kj == qi))(functools.partial(step_body, seq_ends, "diagonal"))
        pl.when(ends & (kj < 0))(functools.partial(step_body, seq_ends, "none"))

    @pl.when(g == last)
    def _():
        for c in page_copies(last, 1 - slot):
            c.wait()


def _attend(page_table, kT_cache, vT_cache, decay, q, kaug, v16, k16, kb16, qT16, vT16):
    nb, n_pages = page_table.shape
    bsz, seq, _ = k16.shape
    npg = PAGES_PER_STEP
    steps_per_seq = n_pages // npg
    steps = nb * steps_per_seq
    t = _DecodeOps.TOKENS
    nq = seq // Q_TILE
    items = [(qi, kj) for qi in range(nq) for kj in range(qi + 1)]
    pairs = min(bsz * B_HEADS, steps // len(items))
    work = [(p // B_HEADS, p % B_HEADS, qi, kj) for p in range(pairs) for qi, kj in items]
    work += [((pairs - 1) // B_HEADS, (pairs - 1) % B_HEADS, 0, -1)] * (steps - len(work))
    work = jnp.asarray(work, jnp.int32).T

    per_seq = lambda width: pl.BlockSpec((1, t, width), lambda g, pt, wk: (g // steps_per_seq, 0, 0))
    in_specs = ([pl.BlockSpec(memory_space=pl.ANY), pl.BlockSpec(memory_space=pl.ANY)]
                + [pl.BlockSpec((1, B_HEADS, npg * PAGE_SIZE),
                                lambda g, pt, wk: (g // steps_per_seq, 0, g % steps_per_seq))]
                + [per_seq(B_WIDTH), per_seq(B_WIDTH + LANES), per_seq(B_WIDTH)]
                + [pl.BlockSpec((1, seq, LANES), lambda g, pt, wk: (wk[0, g], 0, wk[1, g] // 2)),
                   pl.BlockSpec((1, seq, LANES), lambda g, pt, wk: (wk[0, g], 0, 0)),
                   pl.BlockSpec((1, LANES, seq), lambda g, pt, wk: (wk[0, g], wk[1, g] // 2, 0)),
                   pl.BlockSpec((1, B_HEAD_DIM, seq), lambda g, pt, wk: (wk[0, g], wk[1, g], 0))])
    grid_spec = pltpu.PrefetchScalarGridSpec(
        num_scalar_prefetch=2,
        grid=(steps,),
        in_specs=in_specs,
        out_specs=(per_seq(B_WIDTH),
                   pl.BlockSpec((1, B_HEAD_DIM, seq), lambda g, pt, wk: (wk[0, g], wk[1, g], 0))),
        scratch_shapes=[pltpu.VMEM((B_HEADS * t, B_WIDTH + LANES), bf16),
                        pltpu.VMEM((B_HEADS * t, B_WIDTH), f32),
                        pltpu.VMEM((B_HEADS * t, 1), f32), pltpu.VMEM((B_HEADS * t, 1), f32),
                        pltpu.VMEM((2 * LANES, Q_TILE), bf16), pltpu.VMEM((B_HEAD_DIM + SUM_ROWS, Q_TILE), f32),
                        pltpu.VMEM((1, Q_TILE), f32),
                        pltpu.VMEM((2, npg, B_WIDTH, PAGE_SIZE), f32), pltpu.VMEM((2, npg, B_WIDTH, PAGE_SIZE), f32),
                        pltpu.SemaphoreType.DMA((2, 2))],
    )
    o_dec, oT = pl.pallas_call(
        functools.partial(_attend_kernel, steps_per_seq),
        grid_spec=grid_spec,
        out_shape=(jax.ShapeDtypeStruct((nb, t, B_WIDTH), f32), jax.ShapeDtypeStruct((bsz, B_WIDTH, seq), f32)),
        compiler_params=pltpu.CompilerParams(dimension_semantics=("arbitrary",), vmem_limit_bytes=VMEM_LIMIT),
        name="attend",
    )(page_table, work, kT_cache, vT_cache, decay,
      q.reshape(nb, t, B_WIDTH), kaug.reshape(nb, t, B_WIDTH + LANES), v16.reshape(nb, t, B_WIDTH),
      k16, kb16, qT16, vT16)
    return o_dec, oT, pairs


def _finish_kernel(transposed, x_ref, ga_ref, o_ref, gzb_ref, p_ref, wo_ref, gpost_ref, wpg_ref, bpg_ref,
                   wpe_ref, y_ref):
    batched = x_ref.ndim == 3
    ld = (lambda r: r[0]) if batched else (lambda r: r[...])
    x = ld(x_ref)
    o = ld(o_ref)
    if transposed:
        o = o.T
    mix_b = (o * ld(gzb_ref)).astype(bf16)
    y = (jnp.dot(ld(ga_ref), wo_ref[0:A_WIDTH, :], preferred_element_type=f32)
         + jnp.dot(mix_b, wo_ref[A_WIDTH:, :], preferred_element_type=f32))
    ms = jnp.mean(y * y, axis=-1, keepdims=True)
    x = x + y * lax.rsqrt(ms + RMS_EPS) * gpost_ref[...]
    gate = jax.nn.sigmoid(jnp.dot(x.astype(bf16), wpg_ref[...], preferred_element_type=f32) + bpg_ref[...])
    res = x + gate * jnp.dot(ld(p_ref).astype(bf16), wpe_ref[...], preferred_element_type=f32)
    if batched:
        y_ref[0] = res
    else:
        y_ref[...] = res


def _finish_prompt(x, ga, oT, gzb, p, wo16, gpost, wpg16, bpg, wpe16):
    bsz, seq, _ = x.shape
    rows = PROJ_ROWS
    const = lambda *shape: pl.BlockSpec(shape, lambda b, i: (0,) * len(shape))
    tile_rows = lambda width: pl.BlockSpec((1, rows, width), lambda b, i: (b, i, 0))
    return pl.pallas_call(
        functools.partial(_finish_kernel, True),
        grid=(bsz, seq // rows),
        in_specs=[tile_rows(D_MODEL), tile_rows(A_WIDTH),
                  pl.BlockSpec((1, B_WIDTH, rows), lambda b, i: (b, 0, i)),
                  tile_rows(B_WIDTH), tile_rows(PLE_DIM), const(D_MODEL, D_MODEL), const(1, D_MODEL),
                  const(D_MODEL, D_MODEL), const(1, D_MODEL), const(PLE_DIM, D_MODEL)],
        out_specs=tile_rows(D_MODEL),
        out_shape=jax.ShapeDtypeStruct(x.shape, f32),
        compiler_params=pltpu.CompilerParams(dimension_semantics=("arbitrary", "arbitrary"),
                                             vmem_limit_bytes=VMEM_LIMIT),
        name="finish_prompt",
    )(x, ga, oT, gzb, p, wo16, gpost, wpg16, bpg, wpe16)


def _finish_sample(x, ga, o, gzb, p, wo16, gpost, wpg16, bpg, wpe16):
    n = x.shape[0]
    rows = PROJ_ROWS
    const = lambda *shape: pl.BlockSpec(shape, lambda i: (0,) * len(shape))
    tile = lambda width: pl.BlockSpec((rows, width), lambda i: (i, 0))
    return pl.pallas_call(
        functools.partial(_finish_kernel, False),
        grid=(n // rows,),
        in_specs=[tile(D_MODEL), tile(A_WIDTH), tile(B_WIDTH), tile(B_WIDTH), tile(PLE_DIM),
                  const(D_MODEL, D_MODEL), const(1, D_MODEL), const(D_MODEL, D_MODEL), const(1, D_MODEL),
                  const(PLE_DIM, D_MODEL)],
        out_specs=tile(D_MODEL),
        out_shape=jax.ShapeDtypeStruct(x.shape, f32),
        compiler_params=pltpu.CompilerParams(dimension_semantics=("arbitrary",),
                                             vmem_limit_bytes=VMEM_LIMIT),
        name="finish_sample",
    )(x, ga, o, gzb, p, wo16, gpost, wpg16, bpg, wpe16)


def kernel(x_prompt, x_sample, cache_k, cache_v, cache_logf, page_table, p_prompt, p_sample, ln_pre_g, w_in, b_f,
           ln_v_g, ln_v_b, w_s, b_s, w_out, ln_post_g, w_pe, w_pg, b_pg):
    assert w_in.shape[0] == 1, "single-layer step"
    bsz, seq, _ = x_prompt.shape
    nb, t, _ = x_sample.shape
    n_pool = cache_k.shape[1]

    w16 = w_in[0, :, :MAIN_WIDTH].astype(bf16)
    wf16 = jnp.pad(w_in[0, :, MAIN_WIDTH:], ((0, 0), (0, LANES - B_HEADS))).astype(bf16)
    bfp = jnp.pad(b_f[0], (0, LANES - B_HEADS)).reshape(1, LANES)
    g_pre = ln_pre_g[0].reshape(1, D_MODEL)
    lng = ln_v_g[0].reshape(1, A_WIDTH)
    lnb = ln_v_b[0].reshape(1, A_WIDTH)
    bsT = b_s[0].T
    wo16 = w_out[0].astype(bf16)
    wpg16 = w_pg[0].astype(bf16)
    wpe16 = w_pe[0].astype(bf16)
    gpost = ln_post_g[0].reshape(1, D_MODEL)
    bpg = b_pg[0].reshape(1, D_MODEL)

    kT, vT, lfT, qT16, k16, kb16, vT16, ga, gzb = _proj_prompt(x_prompt, g_pre, w16, wf16, bfp, lng, lnb, w_s[0], bsT)
    new_k_prompt = kT.reshape(bsz, B_HEADS, B_HEAD_DIM, seq).transpose(0, 3, 1, 2)[None]
    new_v_prompt = vT.reshape(bsz, B_HEADS, B_HEAD_DIM, seq).transpose(0, 3, 1, 2)[None]
    new_logf_prompt = lfT.transpose(0, 2, 1)[None]

    kT_cache = cache_k[0].transpose(0, 2, 3, 1).reshape(n_pool, B_WIDTH, PAGE_SIZE)
    vT_cache = cache_v[0].transpose(0, 2, 3, 1).reshape(n_pool, B_WIDTH, PAGE_SIZE)
    lf_cache = cache_logf[0].transpose(0, 2, 1)
    xs = x_sample.reshape(nb * t, D_MODEL)
    ks, vs, lfs, vns, qs, kaug, vs16, gas, gzbs = _proj_sample(xs, g_pre, w16, wf16, bfp, lng, lnb, w_s[0], bsT)
    decay = _past_decay(page_table, lf_cache)
    o_s, oT, pairs = _attend(page_table, kT_cache, vT_cache, decay, qs, kaug, vs16, k16, kb16, qT16, vT16)
    for b0 in range(pairs // B_HEADS, bsz):
        h0 = pairs % B_HEADS if b0 == pairs // B_HEADS else 0
        oT = _flash_heads(k16, kb16, qT16, vT16, oT, b0, h0, B_HEADS - h0)
    y_prompt = _finish_prompt(x_prompt, ga, oT, gzb, p_prompt[0], wo16, gpost, wpg16, bpg, wpe16)
    y_sample = _finish_sample(xs, gas, o_s.reshape(nb * t, B_WIDTH), gzbs, p_sample[0].reshape(nb * t, PLE_DIM),
                              wo16, gpost, wpg16, bpg, wpe16).reshape(nb, t, D_MODEL)

    return (y_prompt, y_sample, new_k_prompt, new_v_prompt, new_logf_prompt,
            ks.reshape(1, nb, t, B_HEADS, B_HEAD_DIM), vs.reshape(1, nb, t, B_HEADS, B_HEAD_DIM),
            lfs[:, :B_HEADS].reshape(1, nb, t, B_HEADS), vns.reshape(1, nb, t, A_GROUPS, CHUNK))
```
